```python
import math
import jax, jax.numpy as jnp
from jax import lax
import numpy as np

D_MODEL = 1024
BATCH = 8
SEQ = 2048
DEPTH = 4
DEC_BATCH = 32
DEC_SEQ = 4
PAST_LEN = 8192
PAGE_SIZE = 128

N_EVEN = (DEPTH + 1) // 2
N_ODD = DEPTH // 2
A_HD = 64
A_HEADS = (D_MODEL // 2) // (2 * A_HD)
A_WIDTH = A_HEADS * 2 * A_HD
A_COLS = 3 * A_WIDTH
A_SCALE = A_HD ** -0.5
B_HD = 64
B_WIDTH = D_MODEL // 2
B_HEADS = B_WIDTH // B_HD
DECAY_RANK = 64
ICLR_RANK = 64
GATE_RANK = 128
B_COLS = 3 * B_WIDTH + DECAY_RANK + ICLR_RANK + GATE_RANK
B_SPLITS = (B_WIDTH, 2 * B_WIDTH, 3 * B_WIDTH, 3 * B_WIDTH + DECAY_RANK, 3 * B_WIDTH + DECAY_RANK + ICLR_RANK)
IN_COLS = A_COLS + B_COLS
MIX_WIDTH = A_WIDTH + B_WIDTH
POOL_WINDOWS = (2, 4, 8, 16)
POOL_GW = D_MODEL // len(POOL_WINDOWS)
POOL_BUF = max(POOL_WINDOWS) - 1
D_FF = 2816
CONV_W = 3
Q_BLOCK = 128
NORM_EPS = 1e-6
SUBLN_EPS = 1e-5
GN_EPS = 64e-5

kernel_name = 'hybrid_diffattn_rwkv7_pool_convffn_step'


def rmsnorm(x, g, eps=NORM_EPS):
    xf = x.astype(jnp.float32)
    y = xf * lax.rsqrt(jnp.mean(xf * xf, axis=-1, keepdims=True) + eps)
    return (y * g.astype(jnp.float32)).astype(x.dtype)


def diff_attn_prompt(q, k, v, lam):
    Bn, S = q.shape[0], q.shape[1]
    nb = S // Q_BLOCK
    qb = jnp.moveaxis(q.reshape(Bn, nb, Q_BLOCK, A_HEADS, 2, A_HD), 1, 0)
    kpos = jnp.arange(S)

    def block(args):
        qi, i = args
        s = jnp.einsum('bqhcd,bkhcd->bhcqk', qi, k).astype(jnp.float32) * A_SCALE
        qpos = i * Q_BLOCK + jnp.arange(Q_BLOCK)
        s = jnp.where(kpos[None, :] <= qpos[:, None], s, -jnp.inf)
        p = jax.nn.softmax(s, axis=-1)
        wgt = p[:, :, 0] - lam * p[:, :, 1]
        return jnp.einsum('bhqk,bkhe->bqhe', wgt.astype(v.dtype), v)

    o = lax.map(block, (qb, jnp.arange(nb)))
    return jnp.moveaxis(o, 0, 1).reshape(Bn, S, A_HEADS, 2 * A_HD)


def diff_attn_sample(q, k, v, lam, kc, vc, page_table):
    Bn, T = q.shape[0], q.shape[1]
    past = page_table.shape[1] * kc.shape[1]
    kp = kc[page_table].reshape(Bn, past, A_HEADS, 2, A_HD)
    vp = vc[page_table].reshape(Bn, past, A_HEADS, 2 * A_HD)
    s_past = jnp.einsum('bqhcd,bkhcd->bhcqk', q, kp).astype(jnp.float32) * A_SCALE
    s_new = jnp.einsum('bqhcd,bkhcd->bhcqk', q, k).astype(jnp.float32) * A_SCALE
    causal = jnp.tril(jnp.ones((T, T), dtype=bool))
    s_new = jnp.where(causal, s_new, -jnp.inf)
    p = jax.nn.softmax(jnp.concatenate([s_past, s_new], axis=-1), axis=-1)
    wgt = (p[:, :, 0] - lam * p[:, :, 1]).astype(v.dtype)
    return (jnp.einsum('bhqk,bkhe->bqhe', wgt[..., :past], vp)
            + jnp.einsum('bhqk,bkhe->bqhe', wgt[..., past:], v))


def rwkv7_mix(pb, shift_prev, wkv0, W, e):
    Bn, T, _ = pb.shape
    f32 = jnp.float32
    prev = jnp.concatenate([shift_prev[:, None].astype(pb.dtype), pb[:, :-1]], axis=1)
    xm = pb + (prev - pb) * W['rwkv_mu'][e]
    r, k, v, wd, ad, gd = jnp.split(xm, B_SPLITS, axis=-1)
    w = -jax.nn.softplus(-(W['rwkv_w0'][e] + jnp.tanh(wd) @ W['rwkv_w2'][e])) - 0.5
    decay = jnp.exp(-jnp.exp(w.astype(f32)))
    a = jax.nn.sigmoid(W['rwkv_a0'][e] + ad @ W['rwkv_a2'][e])
    g = jax.nn.sigmoid(gd) @ W['rwkv_g2'][e]

    def heads(t):
        return t.reshape(Bn, T, B_HEADS, B_HD).astype(f32)

    r, k, v, a, decay = heads(r), heads(k), heads(v), heads(a), heads(decay)
    kk = k * W['rwkv_k_k'][e].reshape(B_HEADS, B_HD).astype(f32)
    kk = kk / jnp.maximum(jnp.sqrt(jnp.sum(kk * kk, axis=-1, keepdims=True)), 1e-12)
    k = k * (1.0 + (a - 1.0) * W['rwkv_k_a'][e].reshape(B_HEADS, B_HD).astype(f32))

    def step(S, inp):
        r_t, k_t, v_t, d_t, kk_t, a_t = inp
        sa = jnp.einsum('bhvk,bhk->bhv', S, -kk_t)
        S = (S * d_t[:, :, None, :] + sa[..., None] * (kk_t * a_t)[:, :, None, :]
             + v_t[..., None] * k_t[:, :, None, :])
        return S, jnp.einsum('bhvk,bhk->bhv', S, r_t)

    seq = (jnp.moveaxis(r, 1, 0), jnp.moveaxis(k, 1, 0), jnp.moveaxis(v, 1, 0),
           jnp.moveaxis(decay, 1, 0), jnp.moveaxis(kk, 1, 0), jnp.moveaxis(a, 1, 0))
    S_T, o = lax.scan(step, wkv0.astype(f32), seq)
    o = jnp.moveaxis(o, 0, 1)
    mu = jnp.mean(o, axis=-1, keepdims=True)
    var = jnp.mean(jnp.square(o - mu), axis=-1, keepdims=True)
    o = ((o - mu) * lax.rsqrt(var + GN_EPS)).reshape(Bn, T, B_WIDTH)
    o = o * W['rwkv_gn_w'][e].astype(f32) + W['rwkv_gn_b'][e].astype(f32)
    bonus = jnp.sum(r * k * W['rwkv_r_k'][e].astype(f32), axis=-1, keepdims=True) * v
    o = (o + bonus.reshape(Bn, T, B_WIDTH)) * g.astype(f32)
    return o.astype(pb.dtype), S_T.astype(wkv0.dtype), pb[:, -1].astype(shift_prev.dtype)


def pool_mixer(h, prev, pos0, w_pool, scale):
    Bn, T, _ = h.shape
    f32 = jnp.float32
    xs = jnp.concatenate([prev.astype(h.dtype), h], axis=1)
    xf = xs.astype(f32)
    cs = jnp.concatenate([jnp.zeros((Bn, 1, D_MODEL), f32), jnp.cumsum(xf, axis=1)], axis=1)
    pos = pos0 + jnp.arange(T)
    x_self = xf[:, POOL_BUF:]
    hi = cs[:, POOL_BUF + 1:POOL_BUF + 1 + T]
    outs = []
    for gi, win in enumerate(POOL_WINDOWS):
        c0, c1 = gi * POOL_GW, (gi + 1) * POOL_GW
        lo = cs[:, POOL_BUF + 1 - win:POOL_BUF + 1 - win + T, c0:c1]
        cnt = jnp.minimum(pos + 1, win).astype(f32)[None, :, None]
        m = (hi[..., c0:c1] - lo) / cnt - x_self[..., c0:c1]
        outs.append(jnp.einsum('btc,ce->bte', m.astype(h.dtype), w_pool[gi]))
    y = jnp.concatenate(outs, axis=-1) * scale
    return y, xs[:, -POOL_BUF:]


def conv_ffn(h, prev, w_up, conv_w, conv_b, w_down):
    T = h.shape[1]
    u = h @ w_up
    gate, val = u[..., :D_FF], u[..., D_FF:]
    gp = jnp.concatenate([prev.astype(gate.dtype), gate], axis=1)
    c = conv_b
    for j in range(CONV_W):
        c = c + gp[:, j:j + T] * conv_w[j]
    y = (jax.nn.gelu(c, approximate=False) * val) @ w_down
    return y, gp[:, -(CONV_W - 1):]


def trunk(x, pos0, attn_fn, wkv_in, shift_in, pool_in, conv_in, W):
    Bn, T, _ = x.shape
    k_rows, v_rows, wkv_out, shift_out, pool_out, conv_out = [], [], [], [], [], []
    for l in range(DEPTH):
        h = rmsnorm(x, W['norm_mix'][l])
        if l % 2 == 0:
            e = l // 2
            lam_init = 0.8 - 0.6 * math.exp(-0.3 * l)
            p = h @ W['w_in'][e]
            q = p[..., :A_WIDTH].reshape(Bn, T, A_HEADS, 2, A_HD)
            k = p[..., A_WIDTH:2 * A_WIDTH].reshape(Bn, T, A_HEADS, 2, A_HD)
            v = p[..., 2 * A_WIDTH:A_COLS].reshape(Bn, T, A_HEADS, 2 * A_HD)
            lq = W['diff_lambda'][e].astype(jnp.float32)
            lam = jnp.exp(jnp.sum(lq[0] * lq[1])) - jnp.exp(jnp.sum(lq[2] * lq[3])) + lam_init
            oa = attn_fn(e, q, k, v, lam)
            oa = rmsnorm(oa, W['diff_subln'][e], SUBLN_EPS) * (1.0 - lam_init)
            ob, wkv_new, shift_new = rwkv7_mix(p[..., A_COLS:], shift_in[e], wkv_in[e], W, e)
            mix = jnp.concatenate([oa.reshape(Bn, T, A_WIDTH).astype(x.dtype), ob], axis=-1)
            x = x + mix @ W['w_out'][e]
            k_rows.append(k.reshape(Bn, T, A_HEADS, 2 * A_HD))
            v_rows.append(v)
            wkv_out.append(wkv_new)
            shift_out.append(shift_new)
        else:
            o = l // 2
            y, buf = pool_mixer(h, pool_in[o], pos0, W['pool_w'][o], W['pool_scale'][o])
            x = x + y
            pool_out.append(buf)
        h = rmsnorm(x, W['norm_ffn'][l])
        y, cbuf = conv_ffn(h, conv_in[l], W['ffn_up'][l], W['ffn_conv_w'][l], W['ffn_conv_b'][l], W['ffn_down'][l])
        x = x + y
        conv_out.append(cbuf)
    y = rmsnorm(x, W['norm_final'])
    return (y, jnp.stack(k_rows), jnp.stack(v_rows), jnp.stack(wkv_out), jnp.stack(shift_out),
            jnp.stack(pool_out), jnp.stack(conv_out))


def setup_inputs(seed: int = 0) -> dict:
    key = jax.random.key(seed)
    ks = jax.random.split(key, 40)
    f32 = jnp.float32
    n_pages = PAST_LEN // PAGE_SIZE
    n_pool = (DEC_BATCH * n_pages * 5) // 4

    def nrm(k, shape, s=1.0):
        return jax.random.normal(k, shape, f32) * s

    page_table = jax.random.permutation(ks[0], n_pool)[:DEC_BATCH * n_pages]
    page_table = page_table.reshape(DEC_BATCH, n_pages).astype(jnp.int32)
    return {
        'x_prompt': nrm(ks[1], (BATCH, SEQ, D_MODEL)),
        'x_sample': nrm(ks[2], (DEC_BATCH, DEC_SEQ, D_MODEL)),
        'cache_k': nrm(ks[3], (N_EVEN, n_pool, PAGE_SIZE, A_HEADS, 2 * A_HD)),
        'cache_v': nrm(ks[4], (N_EVEN, n_pool, PAGE_SIZE, A_HEADS, 2 * A_HD)),
        'state_wkv': nrm(ks[5], (N_EVEN, DEC_BATCH, B_HEADS, B_HD, B_HD), 0.5),
        'state_shift': nrm(ks[6], (N_EVEN, DEC_BATCH, B_COLS)),
        'state_pool': nrm(ks[7], (N_ODD, DEC_BATCH, POOL_BUF, D_MODEL)),
        'state_ffn_conv': nrm(ks[8], (DEPTH, DEC_BATCH, CONV_W - 1, D_FF)),
        'page_table': page_table,
        'norm_mix': 1.0 + nrm(ks[9], (DEPTH, D_MODEL), 0.02),
        'norm_ffn': 1.0 + nrm(ks[10], (DEPTH, D_MODEL), 0.02),
        'norm_final': 1.0 + nrm(ks[11], (D_MODEL,), 0.02),
        'w_in': nrm(ks[12], (N_EVEN, D_MODEL, IN_COLS), D_MODEL ** -0.5),
        'w_out': nrm(ks[13], (N_EVEN, MIX_WIDTH, D_MODEL), MIX_WIDTH ** -0.5),
        'diff_lambda': nrm(ks[14], (N_EVEN, 4, A_HD), 0.1),
        'diff_subln': 1.0 + nrm(ks[15], (N_EVEN, 2 * A_HD), 0.02),
        'rwkv_mu': jax.random.uniform(ks[16], (N_EVEN, B_COLS), f32),
        'rwkv_w0': jax.random.uniform(ks[17], (N_EVEN, B_WIDTH), f32, -6.0, 1.0),
        'rwkv_w2': nrm(ks[18], (N_EVEN, DECAY_RANK, B_WIDTH), DECAY_RANK ** -0.5),
        'rwkv_a0': nrm(ks[19], (N_EVEN, B_WIDTH), 0.1),
        'rwkv_a2': nrm(ks[20], (N_EVEN, ICLR_RANK, B_WIDTH), ICLR_RANK ** -0.5),
        'rwkv_g2': nrm(ks[21], (N_EVEN, GATE_RANK, B_WIDTH), GATE_RANK ** -0.5),
        'rwkv_k_k': 0.85 + nrm(ks[22], (N_EVEN, B_WIDTH), 0.02),
        'rwkv_k_a': 1.0 + nrm(ks[23], (N_EVEN, B_WIDTH), 0.02),
        'rwkv_r_k': nrm(ks[24], (N_EVEN, B_HEADS, B_HD), 0.1),
        'rwkv_gn_w': 1.0 + nrm(ks[25], (N_EVEN, B_WIDTH), 0.02),
        'rwkv_gn_b': nrm(ks[26], (N_EVEN, B_WIDTH), 0.01),
        'pool_w': nrm(ks[27], (N_ODD, len(POOL_WINDOWS), POOL_GW, POOL_GW), POOL_GW ** -0.5),
        'pool_scale': 1.0 + nrm(ks[28], (N_ODD, D_MODEL), 0.02),
        'ffn_up': nrm(ks[29], (DEPTH, D_MODEL, 2 * D_FF), D_MODEL ** -0.5),
        'ffn_conv_w': nrm(ks[30], (DEPTH, CONV_W, D_FF), CONV_W ** -0.5),
        'ffn_conv_b': nrm(ks[31], (DEPTH, D_FF), 0.01),
        'ffn_down': nrm(ks[32], (DEPTH, D_FF, D_MODEL), D_FF ** -0.5),
    }


def reference(x_prompt, x_sample, cache_k, cache_v, state_wkv, state_shift, state_pool, state_ffn_conv,
              page_table, norm_mix, norm_ffn, norm_final, w_in, w_out, diff_lambda, diff_subln,
              rwkv_mu, rwkv_w0, rwkv_w2, rwkv_a0, rwkv_a2, rwkv_g2, rwkv_k_k, rwkv_k_a, rwkv_r_k,
              rwkv_gn_w, rwkv_gn_b, pool_w, pool_scale, ffn_up, ffn_conv_w, ffn_conv_b, ffn_down):
    W = dict(norm_mix=norm_mix, norm_ffn=norm_ffn, norm_final=norm_final, w_in=w_in, w_out=w_out,
             diff_lambda=diff_lambda, diff_subln=diff_subln, rwkv_mu=rwkv_mu, rwkv_w0=rwkv_w0,
             rwkv_w2=rwkv_w2, rwkv_a0=rwkv_a0, rwkv_a2=rwkv_a2, rwkv_g2=rwkv_g2, rwkv_k_k=rwkv_k_k,
             rwkv_k_a=rwkv_k_a, rwkv_r_k=rwkv_r_k, rwkv_gn_w=rwkv_gn_w, rwkv_gn_b=rwkv_gn_b,
             pool_w=pool_w, pool_scale=pool_scale, ffn_up=ffn_up, ffn_conv_w=ffn_conv_w,
             ffn_conv_b=ffn_conv_b, ffn_down=ffn_down)
    past_len = page_table.shape[1] * cache_k.shape[2]
    Bp = x_prompt.shape[0]
    dt = x_prompt.dtype
    wkv0 = jnp.zeros((N_EVEN, Bp, B_HEADS, B_HD, B_HD), dt)
    shift0 = jnp.zeros((N_EVEN, Bp, B_COLS), dt)
    pool0 = jnp.zeros((N_ODD, Bp, POOL_BUF, D_MODEL), dt)
    conv0 = jnp.zeros((DEPTH, Bp, CONV_W - 1, D_FF), dt)

    def prompt_attn(e, q, k, v, lam):
        return diff_attn_prompt(q, k, v, lam)

    def sample_attn(e, q, k, v, lam):
        return diff_attn_sample(q, k, v, lam, cache_k[e], cache_v[e], page_table)

    y_prompt, k_p, v_p, wkv_p, shift_p, pool_p, conv_p = trunk(
        x_prompt, 0, prompt_attn, wkv0, shift0, pool0, conv0, W)
    y_sample, k_s, v_s, wkv_s, shift_s, pool_s, conv_s = trunk(
        x_sample, past_len, sample_attn, state_wkv, state_shift, state_pool, state_ffn_conv, W)
    return (y_prompt, y_sample, k_p, v_p, wkv_p, shift_p, pool_p, conv_p,
            k_s, v_s, wkv_s, shift_s, pool_s, conv_s)
```

```python
import functools
import math

import jax
import jax.numpy as jnp
from jax import lax
from jax.experimental import pallas as pl
from jax.experimental.pallas import tpu as pltpu

F32 = jnp.float32
BF16 = jnp.bfloat16

D_MODEL = 1024
DEPTH = 4
PAGE_SIZE = 128
A_HD = 64
A_HEADS = 4
A_WIDTH = 512
A_COLS = 3 * A_WIDTH
A_SCALE = A_HD ** -0.5
B_HD = 64
B_WIDTH = 512
B_HEADS = 8
DECAY_RANK = 64
ICLR_RANK = 64
GATE_RANK = 128
B_COLS = 3 * B_WIDTH + DECAY_RANK + ICLR_RANK + GATE_RANK
IN_COLS = A_COLS + B_COLS
POOL_WINDOWS = (2, 4, 8, 16)
POOL_GW = D_MODEL // len(POOL_WINDOWS)
POOL_BUF = max(POOL_WINDOWS) - 1
D_FF = 2816
CONV_W = 3
NORM_EPS = 1e-6
SUBLN_EPS = 1e-5
GN_EPS = 64e-5

LANES = 128
SUBLANES = 8
VMEM_LIMIT = 56 * 1024 * 1024
FF_CHUNK = 256
PAGES_PER_STEP = 8
NEW_KEY_ROWS = 16


def _div_nonneg(x, n):
    if n == 1:
        return x
    if n & (n - 1) == 0:
        return lax.shift_right_logical(x, n.bit_length() - 1)
    return x // n


def _cparams(sem):
    return pltpu.CompilerParams(dimension_semantics=sem, vmem_limit_bytes=VMEM_LIMIT)


def _const_spec(shape):
    nd = len(shape)
    return pl.BlockSpec(shape, lambda *_: (0,) * nd, pipeline_mode=pl.Buffered(1))


def _rms(x, g, eps):
    return x * lax.rsqrt(jnp.mean(x * x, axis=-1, keepdims=True) + eps) * g


def _split_dot(x, m):
    hi = x.astype(BF16)
    lo = (x - hi.astype(F32)).astype(BF16)
    return (jnp.dot(hi, m, preferred_element_type=F32)
            + jnp.dot(lo, m, preferred_element_type=F32))


def _softplus(z):
    return jnp.maximum(z, 0.0) + jnp.log1p(jnp.exp(-jnp.abs(z)))


def _inproj_kernel(x_ref, gn_ref, win_ref, sprev_ref, mu_ref, w0_ref, w2p_ref, a0_ref, a2p_ref,
                   g2_ref, kk_ref, ka_ref, rk_ref, seg_ref,
                   k_ref, v_ref, qkv_ref, sout_ref, nkk_ref, d_ref, b_ref, kt_ref, r_ref,
                   vb_ref, g_ref, bonus_ref, ext_ref, *, tm, stride, halo):
    s = pl.program_id(1)
    ns = pl.num_programs(1)
    h = _rms(x_ref[...], gn_ref[...], NORM_EPS).astype(BF16)
    p = jnp.dot(h, win_ref[...], preferred_element_type=F32)
    k_ref[...] = p[:, A_WIDTH:2 * A_WIDTH]
    v_ref[...] = p[:, 2 * A_WIDTH:A_COLS]
    qkv_ref[:, :A_WIDTH] = (p[:, :A_WIDTH] * A_SCALE).astype(BF16)
    qkv_ref[:, A_WIDTH:] = p[:, A_WIDTH:A_COLS].astype(BF16)
    pb = p[:, A_COLS:]

    @pl.when(s == 0)
    def _():
        ext_ref[halo - stride:halo, :] = sprev_ref[...]

    ext_ref[halo:halo + tm, :] = pb
    prev = ext_ref[halo - stride:halo - stride + tm, :]
    ext_ref[halo - stride:halo, :] = ext_ref[halo + tm - stride:halo + tm, :]

    @pl.when(s == ns - 1)
    def _():
        sout_ref[...] = ext_ref[halo + tm - stride:halo + tm, :]

    xm = pb + (prev - pb) * mu_ref[...]
    r = xm[:, :B_WIDTH]
    k = xm[:, B_WIDTH:2 * B_WIDTH]
    v = xm[:, 2 * B_WIDTH:3 * B_WIDTH]
    wa = xm[:, 3 * B_WIDTH:3 * B_WIDTH + DECAY_RANK + ICLR_RANK]
    gd = xm[:, 3 * B_WIDTH + DECAY_RANK + ICLR_RANK:]
    wlin = w0_ref[...] + jnp.dot(jnp.tanh(wa).astype(BF16), w2p_ref[...], preferred_element_type=F32)
    w = -_softplus(-wlin) - 0.5
    decay = jnp.exp(-jnp.exp(w))
    a = jax.nn.sigmoid(a0_ref[...] + jnp.dot(wa.astype(BF16), a2p_ref[...], preferred_element_type=F32))
    g = jnp.dot(jax.nn.sigmoid(gd).astype(BF16), g2_ref[...], preferred_element_type=F32)
    kk = k * kk_ref[...]
    ss = _split_dot(kk * kk, seg_ref[...])
    kk = kk / jnp.maximum(jnp.sqrt(ss), 1e-12)
    kt = k * (1.0 + (a - 1.0) * ka_ref[...])
    bonus = _split_dot(r * kt * rk_ref[...], seg_ref[...]) * v
    nkk_ref[...] = -kk
    d_ref[...] = decay
    b_ref[...] = kk * a
    kt_ref[...] = kt
    r_ref[...] = r
    vb_ref[...] = v
    g_ref[...] = g
    bonus_ref[...] = bonus


def _inproj(x, gn, win, sprev, prm, *, tm, stride):
    G, T, _ = x.shape
    halo = max(SUBLANES, stride)
    ns = T // tm
    row = lambda w: pl.BlockSpec((None, tm, w), lambda g, s: (g, s, 0))
    stream = jax.ShapeDtypeStruct((G, T, B_WIDTH), F32)
    in_specs = [row(D_MODEL), _const_spec((1, D_MODEL)), _const_spec((D_MODEL, IN_COLS)),
                pl.BlockSpec((None, stride, B_COLS), lambda g, s: (g, 0, 0)),
                _const_spec((1, B_COLS)), _const_spec((1, B_WIDTH)), _const_spec((LANES, B_WIDTH)),
                _const_spec((1, B_WIDTH)), _const_spec((LANES, B_WIDTH)), _const_spec((GATE_RANK, B_WIDTH)),
                _const_spec((1, B_WIDTH)), _const_spec((1, B_WIDTH)), _const_spec((1, B_WIDTH)),
                _const_spec((B_WIDTH, B_WIDTH))]
    out_shape = [stream, stream, jax.ShapeDtypeStruct((G, T, A_COLS), BF16),
                 jax.ShapeDtypeStruct((G, stride, B_COLS), F32)] + [stream] * 8
    out_specs = [row(A_WIDTH), row(A_WIDTH), row(A_COLS),
                 pl.BlockSpec((None, stride, B_COLS), lambda g, s: (g, 0, 0))] + [row(B_WIDTH)] * 8
    return pl.pallas_call(
        functools.partial(_inproj_kernel, tm=tm, stride=stride, halo=halo),
        grid=(G, ns), in_specs=in_specs, out_specs=out_specs, out_shape=out_shape,
        scratch_shapes=[pltpu.VMEM((halo + tm, B_COLS), F32)],
        compiler_params=_cparams(("arbitrary", "arbitrary")), name="inproj",
    )(x, gn, win, sprev, prm["mu"], prm["w0"], prm["w2p"], prm["a0"], prm["a2p"], prm["g2"],
      prm["k_k"], prm["k_a"], prm["r_k"], prm["seg"])


def _lambda_full(lq, lam_init):
    s01 = jnp.sum(lq[0:1, :] * lq[1:2, :], axis=-1, keepdims=True)
    s23 = jnp.sum(lq[2:3, :] * lq[3:4, :], axis=-1, keepdims=True)
    return jnp.exp(s01) - jnp.exp(s23) + lam_init


def _pattn_kernel(q_ref, k_ref, v_ref, lq_ref, sub_ref, o_ref,
                  m0_ref, l0_ref, a0_ref, m1_ref, l1_ref, a1_ref, *, tq, lam_init):
    qi = pl.program_id(2)
    kj = pl.program_id(3)

    @pl.when(kj == 0)
    def _():
        for m_ref, l_ref, a_ref in ((m0_ref, l0_ref, a0_ref), (m1_ref, l1_ref, a1_ref)):
            m_ref[...] = jnp.full(m_ref.shape, -jnp.inf, F32)
            l_ref[...] = jnp.zeros(l_ref.shape, F32)
            a_ref[...] = jnp.zeros(a_ref.shape, F32)

    @pl.when(kj <= qi)
    def _():
        q = q_ref[...]
        k = k_ref[...]
        v = v_ref[...]
        lane = lax.broadcasted_iota(jnp.int32, q.shape, 1)
        zero = jnp.zeros_like(q)
        rows = qi * tq + lax.broadcasted_iota(jnp.int32, (tq, tq), 0)
        cols = kj * tq + lax.broadcasted_iota(jnp.int32, (tq, tq), 1)
        keep = cols <= rows
        for c, (m_ref, l_ref, a_ref) in enumerate(((m0_ref, l0_ref, a0_ref), (m1_ref, l1_ref, a1_ref))):
            qc = jnp.where((lane >= c * A_HD) & (lane < (c + 1) * A_HD), q, zero)
            s = lax.dot_general(qc, k, (((1,), (1,)), ((), ())), preferred_element_type=F32)
            s = jnp.where(keep, s, -jnp.inf)
            m_old = m_ref[...]
            m_new = jnp.maximum(m_old, jnp.max(s, axis=-1, keepdims=True))
            alpha = jnp.exp(m_old - m_new)
            p = jnp.exp(s - m_new)
            l_ref[...] = alpha * l_ref[...] + jnp.sum(p, axis=-1, keepdims=True)
            a_ref[...] = alpha * a_ref[...] + jnp.dot(p.astype(BF16), v, preferred_element_type=F32)
            m_ref[...] = m_new

    @pl.when(kj == qi)
    def _():
        lam = _lambda_full(lq_ref[...], lam_init)
        o = a0_ref[...] / l0_ref[...] - lam * (a1_ref[...] / l1_ref[...])
        o = _rms(o, sub_ref[...], SUBLN_EPS) * (1.0 - lam_init)
        o_ref[...] = o.astype(o_ref.dtype)


def _pattn(qkv, lq, sub, lam_init, *, tq):
    B, S, _ = qkv.shape
    nq = S // tq
    nh = A_HEADS
    q_spec = pl.BlockSpec((None, tq, LANES), lambda b, h, i, j: (b, i, h))
    k_spec = pl.BlockSpec((None, tq, LANES), lambda b, h, i, j: (b, jnp.minimum(i, j), nh + h))
    v_spec = pl.BlockSpec((None, tq, LANES), lambda b, h, i, j: (b, jnp.minimum(i, j), 2 * nh + h))
    return pl.pallas_call(
        functools.partial(_pattn_kernel, tq=tq, lam_init=lam_init),
        grid=(B, nh, nq, nq),
        in_specs=[q_spec, k_spec, v_spec,
                  pl.BlockSpec((4, A_HD), lambda b, h, i, j: (0, 0)),
                  pl.BlockSpec((1, LANES), lambda b, h, i, j: (0, 0))],
        out_specs=pl.BlockSpec((None, tq, LANES), lambda b, h, i, j: (b, i, h)),
        out_shape=jax.ShapeDtypeStruct((B, S, A_WIDTH), BF16),
        scratch_shapes=[pltpu.VMEM((tq, 1), F32), pltpu.VMEM((tq, 1), F32), pltpu.VMEM((tq, LANES), F32),
                        pltpu.VMEM((tq, 1), F32), pltpu.VMEM((tq, 1), F32), pltpu.VMEM((tq, LANES), F32)],
        compiler_params=_cparams(("arbitrary",) * 4), name="pattn",
    )(qkv, qkv, qkv, lq, sub)


def _sattn_kernel(pt_ref, q_ref, kn_ref, vn_ref, lq_ref, sub_ref, *rest, npg, nt, lam_init):
    k_refs = rest[:npg]
    v_refs = rest[npg:2 * npg]
    o_ref = rest[2 * npg]
    m_ref, l_ref, acc_ref, fin_ref = rest[2 * npg + 1:]
    j = pl.program_id(1)
    nj = pl.num_programs(1)
    nr = 2 * A_HEADS * nt

    @pl.when(j == 0)
    def _():
        m_ref[...] = jnp.full(m_ref.shape, -jnp.inf, F32)
        l_ref[...] = jnp.zeros(l_ref.shape, F32)
        acc_ref[...] = jnp.zeros(acc_ref.shape, F32)

    rowi = lax.broadcasted_iota(jnp.int32, (nr, A_WIDTH), 0)
    lanei = lax.broadcasted_iota(jnp.int32, (nr, A_WIDTH), 1)
    own = ((lanei // (2 * A_HD) == (rowi // nt) % A_HEADS)
           & ((lanei // A_HD) % 2 == rowi // (A_HEADS * nt)))
    qm = jnp.where(own, q_ref[...], 0.0).astype(BF16)

    def update(s, vs):
        m_old = m_ref[...]
        m_new = jnp.maximum(m_old, jnp.max(s, axis=-1, keepdims=True))
        alpha = jnp.exp(m_old - m_new)
        p = jnp.exp(s - m_new)
        l_ref[...] = alpha * l_ref[...] + jnp.sum(p, axis=-1, keepdims=True)
        acc = alpha * acc_ref[...]
        w = p.shape[1] // len(vs)
        for i, vv in enumerate(vs):
            acc = acc + jnp.dot(p[:, i * w:(i + 1) * w].astype(BF16), vv, preferred_element_type=F32)
        acc_ref[...] = acc
        m_ref[...] = m_new

    nt_dims = (((1,), (1,)), ((), ()))
    s_pages = [lax.dot_general(qm, kr[...].astype(BF16), nt_dims, preferred_element_type=F32)
               for kr in k_refs]
    update(jnp.concatenate(s_pages, axis=1), [vr[...].astype(BF16) for vr in v_refs])

    @pl.when(j == nj - 1)
    def _():
        s_new = lax.dot_general(qm, kn_ref[...].astype(BF16), nt_dims, preferred_element_type=F32)
        key = lax.broadcasted_iota(jnp.int32, s_new.shape, 1)
        tok = lax.broadcasted_iota(jnp.int32, s_new.shape, 0) % nt
        s_new = jnp.where(key <= tok, s_new, -jnp.inf)
        update(s_new, [vn_ref[...].astype(BF16)])
        lam = _lambda_full(lq_ref[...], lam_init)
        half = nr // 2
        o = (acc_ref[0:half, :] / l_ref[0:half, :]
             - lam * (acc_ref[half:nr, :] / l_ref[half:nr, :]))
        hrow = lax.broadcasted_iota(jnp.int32, o.shape, 0) // nt
        hlane = lax.broadcasted_iota(jnp.int32, o.shape, 1) // (2 * A_HD)
        o = jnp.where(hrow == hlane, o, 0.0)
        ms = jnp.sum(o * o, axis=-1, keepdims=True) / (2 * A_HD)
        fin_ref[...] = o * lax.rsqrt(ms + SUBLN_EPS) * sub_ref[...] * (1.0 - lam_init)
        tot = fin_ref[0:nt, :]
        for hh in range(1, A_HEADS):
            tot = tot + fin_ref[hh * nt:(hh + 1) * nt, :]
        o_ref[...] = tot


def _sattn(page_table, q_rep, k_new, v_new, lq, sub4, kc, vc, lam_init):
    Bn, nr, _ = q_rep.shape
    nt = nr // (2 * A_HEADS)
    n_pages = page_table.shape[1]
    npg = PAGES_PER_STEP
    nj = n_pages // npg
    per_b = lambda w: pl.BlockSpec((None, w, A_WIDTH), lambda b, j, pt: (b, 0, 0))

    def page_spec(i):
        return pl.BlockSpec((None, PAGE_SIZE, A_WIDTH), lambda b, j, pt: (pt[b, j * npg + i], 0, 0))

    in_specs = ([per_b(nr), per_b(NEW_KEY_ROWS), per_b(NEW_KEY_ROWS),
                 pl.BlockSpec((4, A_HD), lambda b, j, pt: (0, 0)),
                 pl.BlockSpec((1, A_WIDTH), lambda b, j, pt: (0, 0))]
                + [page_spec(i) for i in range(npg)] * 2)
    grid_spec = pltpu.PrefetchScalarGridSpec(
        num_scalar_prefetch=1, grid=(Bn, nj), in_specs=in_specs,
        out_specs=pl.BlockSpec((None, nt, A_WIDTH), lambda b, j, pt: (b, 0, 0)),
        scratch_shapes=[pltpu.VMEM((nr, 1), F32), pltpu.VMEM((nr, 1), F32),
                        pltpu.VMEM((nr, A_WIDTH), F32), pltpu.VMEM((nr // 2, A_WIDTH), F32)])
    return pl.pallas_call(
        functools.partial(_sattn_kernel, npg=npg, nt=nt, lam_init=lam_init),
        grid_spec=grid_spec, out_shape=jax.ShapeDtypeStruct((Bn, nt, A_WIDTH), F32),
        compiler_params=_cparams(("arbitrary", "arbitrary")), name="sattn",
    )(page_table, q_rep, k_new, v_new, lq, sub4, *([kc] * npg), *([vc] * npg))


def _wkv_kernel(nkk_ref, d_ref, b_ref, kt_ref, r_ref, v_ref, s0_ref, o_ref, st_ref, s_scr, *, tb, nv):
    t0 = pl.program_id(1)

    @pl.when(t0 == 0)
    def _():
        s_scr[...] = s0_ref[...]

    def step(t, carry):
        nkk = nkk_ref[t]
        d = d_ref[t]
        b = b_ref[t]
        kt = kt_ref[t]
        r = r_ref[t]
        dr = d * r
        br = jnp.sum(b * r, axis=0, keepdims=True)
        kr = jnp.sum(kt * r, axis=0, keepdims=True)

        def row(vr, c):
            S = s_scr[vr]
            sa = jnp.sum(S * nkk, axis=0, keepdims=True)
            qo = jnp.sum(S * dr, axis=0, keepdims=True)
            vv = v_ref[t, pl.ds(vr, 1), :]
            s_scr[vr] = S * d + sa * b + vv * kt
            o_ref[t, pl.ds(vr, 1), :] = qo + sa * br + vv * kr
            return c

        return lax.fori_loop(0, nv, row, carry, unroll=2)

    lax.fori_loop(0, tb, step, 0)
    st_ref[...] = s_scr[...]


def _wkv(nkk, d, b, kt, r, v, s0, *, tb):
    LG, T, _, _ = nkk.shape
    nv = v.shape[2]
    kspec = pl.BlockSpec((None, tb, B_HD, LANES), lambda g, t: (g, t, 0, 0))
    vspec = pl.BlockSpec((None, tb, nv, LANES), lambda g, t: (g, t, 0, 0))
    sspec = pl.BlockSpec((None, nv, B_HD, LANES), lambda g, t: (g, 0, 0, 0))
    return pl.pallas_call(
        functools.partial(_wkv_kernel, tb=tb, nv=nv),
        grid=(LG, T // tb), in_specs=[kspec] * 5 + [vspec, sspec],
        out_specs=[vspec, sspec],
        out_shape=[jax.ShapeDtypeStruct((LG, T, nv, LANES), F32),
                   jax.ShapeDtypeStruct((LG, nv, B_HD, LANES), F32)],
        scratch_shapes=[pltpu.VMEM((nv, B_HD, LANES), F32)],
        compiler_params=_cparams(("arbitrary", "arbitrary")), name="wkv",
    )(nkk, d, b, kt, r, v, s0)


def _mixout_kernel(x_ref, oa_ref, o_ref, g_ref, bonus_ref, gnw_ref, gnb_ref, avg_ref, wout_ref, xo_ref):
    o = o_ref[...]
    mu = _split_dot(o, avg_ref[...])
    dlt = o - mu
    var = _split_dot(dlt * dlt, avg_ref[...])
    on = dlt * lax.rsqrt(var + GN_EPS) * gnw_ref[...] + gnb_ref[...]
    ob = ((on + bonus_ref[...]) * g_ref[...]).astype(BF16)
    y = (jnp.dot(oa_ref[...], wout_ref[0:A_WIDTH, :], preferred_element_type=F32)
         + jnp.dot(ob, wout_ref[A_WIDTH:, :], preferred_element_type=F32))
    xo_ref[...] = x_ref[...] + y


def _mixout(x, oa, o, g, bonus, gnw, gnb, avg, wout, *, tm):
    N = x.shape[0]
    row = lambda w: pl.BlockSpec((tm, w), lambda i: (i, 0))
    return pl.pallas_call(
        _mixout_kernel, grid=(N // tm,),
        in_specs=[row(D_MODEL), row(A_WIDTH), row(B_WIDTH), row(B_WIDTH), row(B_WIDTH),
                  _const_spec((1, B_WIDTH)), _const_spec((1, B_WIDTH)),
                  _const_spec((B_WIDTH, B_WIDTH)), _const_spec((D_MODEL, D_MODEL))],
        out_specs=row(D_MODEL), out_shape=jax.ShapeDtypeStruct((N, D_MODEL), F32),
        compiler_params=_cparams(("arbitrary",)), name="mixout",
    )(x, oa, o, g, bonus, gnw, gnb, avg, wout)


def _pool_kernel(x_ref, gn_ref, pw_ref, ps_ref, pprev_ref, xo_ref, pout_ref, ext_ref,
                 *, tm, stride, halo, pos0):
    s = pl.program_id(1)
    ns = pl.num_programs(1)
    keep = POOL_BUF * stride
    x = x_ref[...]
    h = _rms(x, gn_ref[...], NORM_EPS)

    @pl.when(s == 0)
    def _():
        ext_ref[halo - keep:halo, :] = pprev_ref[...]

    ext_ref[halo:halo + tm, :] = h
    t = _div_nonneg(s * tm + lax.broadcasted_iota(jnp.int32, (tm, 1), 0), stride)
    ys = []
    for gi, win in enumerate(POOL_WINDOWS):
        c0, c1 = gi * POOL_GW, (gi + 1) * POOL_GW
        cur = ext_ref[halo:halo + tm, c0:c1]
        wsum = cur
        for jj in range(1, win):
            wsum = wsum + ext_ref[halo - jj * stride:halo - jj * stride + tm, c0:c1]
        cnt = jnp.minimum(pos0 + t + 1, win).astype(F32)
        m = wsum / cnt - cur
        ys.append(jnp.dot(m.astype(BF16), pw_ref[gi], preferred_element_type=F32))
    xo_ref[...] = x + jnp.concatenate(ys, axis=-1) * ps_ref[...]

    @pl.when(s == ns - 1)
    def _():
        pout_ref[...] = ext_ref[halo + tm - keep:halo + tm, :]

    if tm >= keep:
        ext_ref[halo - keep:halo, :] = ext_ref[halo + tm - keep:halo + tm, :]


def _pool(x, gn, pw, ps, pprev, *, tm, stride, pos0):
    G, T, _ = x.shape
    halo = 2 * SUBLANES * stride
    keep = POOL_BUF * stride
    assert T // tm == 1 or tm >= keep
    row = pl.BlockSpec((None, tm, D_MODEL), lambda g, s: (g, s, 0))
    st = pl.BlockSpec((None, keep, D_MODEL), lambda g, s: (g, 0, 0))
    return pl.pallas_call(
        functools.partial(_pool_kernel, tm=tm, stride=stride, halo=halo, pos0=pos0),
        grid=(G, T // tm),
        in_specs=[row, _const_spec((1, D_MODEL)), _const_spec((len(POOL_WINDOWS), POOL_GW, POOL_GW)),
                  _const_spec((1, D_MODEL)), st],
        out_specs=[row, st],
        out_shape=[jax.ShapeDtypeStruct((G, T, D_MODEL), F32), jax.ShapeDtypeStruct((G, keep, D_MODEL), F32)],
        scratch_shapes=[pltpu.VMEM((halo + tm, D_MODEL), F32)],
        compiler_params=_cparams(("arbitrary", "arbitrary")), name="pool",
    )(x, gn, pw, ps, pprev)


def _ffn_kernel(x_ref, gn_ref, wup_ref, cw_ref, cb_ref, wdn_ref, cprev_ref, xo_ref, cout_ref, ext_ref,
                *, tm, stride, halo):
    s = pl.program_id(1)
    ns = pl.num_programs(1)
    keep = (CONV_W - 1) * stride
    x = x_ref[...]
    h = _rms(x, gn_ref[...], NORM_EPS).astype(BF16)

    @pl.when(s == 0)
    def _():
        ext_ref[halo - keep:halo, :] = cprev_ref[...]

    acc = x
    for c in range(D_FF // FF_CHUNK):
        c0, c1 = c * FF_CHUNK, (c + 1) * FF_CHUNK
        gate = jnp.dot(h, wup_ref[:, c0:c1], preferred_element_type=F32)
        val = jnp.dot(h, wup_ref[:, D_FF + c0:D_FF + c1], preferred_element_type=F32)
        ext_ref[halo:halo + tm, c0:c1] = gate
        cc = cb_ref[:, c0:c1]
        for jj in range(CONV_W - 1):
            back = (CONV_W - 1 - jj) * stride
            cc = cc + ext_ref[halo - back:halo - back + tm, c0:c1] * cw_ref[jj:jj + 1, c0:c1]
        cc = cc + gate * cw_ref[CONV_W - 1:CONV_W, c0:c1]
        act = (0.5 * cc * (1.0 + lax.erf(cc * (2.0 ** -0.5))) * val).astype(BF16)
        acc = acc + jnp.dot(act, wdn_ref[c0:c1, :], preferred_element_type=F32)
    xo_ref[...] = acc

    @pl.when(s == ns - 1)
    def _():
        cout_ref[...] = ext_ref[halo + tm - keep:halo + tm, :]

    ext_ref[halo - keep:halo, :] = ext_ref[halo + tm - keep:halo + tm, :]


def _ffn(x, gn, wup, cw, cb, wdn, cprev, *, tm, stride):
    G, T, _ = x.shape
    keep = (CONV_W - 1) * stride
    halo = max(SUBLANES, keep)
    assert tm >= keep
    row = pl.BlockSpec((None, tm, D_MODEL), lambda g, s: (g, s, 0))
    st = pl.BlockSpec((None, keep, D_FF), lambda g, s: (g, 0, 0))
    return pl.pallas_call(
        functools.partial(_ffn_kernel, tm=tm, stride=stride, halo=halo),
        grid=(G, T // tm),
        in_specs=[row, _const_spec((1, D_MODEL)), _const_spec((D_MODEL, 2 * D_FF)),
                  _const_spec((CONV_W, D_FF)), _const_spec((1, D_FF)), _const_spec((D_FF, D_MODEL)), st],
        out_specs=[row, st],
        out_shape=[jax.ShapeDtypeStruct((G, T, D_MODEL), F32), jax.ShapeDtypeStruct((G, keep, D_FF), F32)],
        scratch_shapes=[pltpu.VMEM((halo + tm, D_FF), F32)],
        compiler_params=_cparams(("arbitrary", "arbitrary")), name="ffn",
    )(x, gn, wup, cw, cb, wdn, cprev)


def _final_norm_kernel(x_ref, g_ref, o_ref):
    o_ref[...] = _rms(x_ref[...], g_ref[...], NORM_EPS)


def _final_norm(x, g, *, tm):
    N = x.shape[0]
    row = pl.BlockSpec((tm, D_MODEL), lambda i: (i, 0))
    return pl.pallas_call(
        _final_norm_kernel, grid=(N // tm,), in_specs=[row, _const_spec((1, D_MODEL))], out_specs=row,
        out_shape=jax.ShapeDtypeStruct((N, D_MODEL), F32),
        compiler_params=_cparams(("arbitrary",)), name="final_norm",
    )(x, g)


def _to_wkv_prompt(a):
    Bn, T, _ = a.shape
    a = a.reshape(Bn, T, B_HEADS, B_HD).transpose(1, 3, 0, 2).reshape(T, B_HD, Bn * B_HEADS)
    return jnp.concatenate([a, a], axis=-1)[None]


def _v_to_wkv_prompt(a):
    Bn, T, _ = a.shape
    a = a.reshape(Bn, T, B_HEADS, 2, B_HD // 2).transpose(1, 4, 3, 0, 2)
    return a.reshape(1, T, B_HD // 2, 2 * Bn * B_HEADS)


def _o_from_wkv_prompt(o, Bn):
    T = o.shape[1]
    o = o.reshape(T, B_HD // 2, 2, Bn, B_HEADS).transpose(3, 0, 4, 2, 1)
    return o.reshape(Bn * T, B_WIDTH)


def _state_from_wkv_prompt(st, Bn):
    st = st.reshape(B_HD // 2, B_HD, 2, Bn, B_HEADS).transpose(3, 4, 2, 0, 1)
    return st.reshape(Bn, B_HEADS, B_HD, B_HD)


def _to_wkv_sample(a, Bn):
    T = a.shape[0] // Bn
    a = a.reshape(T, Bn, B_HEADS, B_HD).transpose(0, 3, 1, 2).reshape(T, B_HD, Bn * B_HEADS // LANES, LANES)
    return a.transpose(2, 0, 1, 3)


def _o_from_wkv_sample(o, Bn):
    LG, T = o.shape[0], o.shape[1]
    o = o.transpose(1, 2, 0, 3).reshape(T, B_HD, Bn, B_HEADS).transpose(0, 2, 3, 1)
    return o.reshape(T * Bn, B_WIDTH)


def _state_to_wkv_sample(st):
    Bn = st.shape[0]
    st = st.transpose(2, 3, 0, 1).reshape(B_HD, B_HD, Bn * B_HEADS // LANES, LANES)
    return st.transpose(2, 0, 1, 3)


def _state_from_wkv_sample(st, Bn):
    st = st.transpose(1, 2, 0, 3).reshape(B_HD, B_HD, Bn, B_HEADS)
    return st.transpose(2, 3, 0, 1)


def _layer_params(W, e):
    z = jnp.zeros((DECAY_RANK, B_WIDTH), F32)
    head = jnp.arange(B_WIDTH) // B_HD
    same = (head[:, None] == head[None, :])
    return {
        "mu": W["rwkv_mu"][e][None], "w0": W["rwkv_w0"][e][None],
        "w2p": jnp.concatenate([W["rwkv_w2"][e], z], axis=0).astype(BF16),
        "a0": W["rwkv_a0"][e][None],
        "a2p": jnp.concatenate([z, W["rwkv_a2"][e]], axis=0).astype(BF16),
        "g2": W["rwkv_g2"][e].astype(BF16),
        "k_k": W["rwkv_k_k"][e][None], "k_a": W["rwkv_k_a"][e][None],
        "r_k": W["rwkv_r_k"][e].reshape(1, B_WIDTH),
        "seg": same.astype(BF16), "avg": (same.astype(F32) / B_HD).astype(BF16),
    }


def _trunk(x, W, *, prompt, wkv_in, shift_in, pool_in, conv_in, attn_ctx):
    G, T, _ = x.shape
    if prompt:
        Bn, stride, tm_in, tm_ffn, tm_pool, pos0 = G, 1, 256, 512, 512, 0
    else:
        Bn, stride, pos0 = attn_ctx["batch"], attn_ctx["batch"], attn_ctx["past_len"]
        tm_in = tm_ffn = tm_pool = T
    N = G * T
    tm_flat = min(N, 512)
    k_rows, v_rows, wkv_out, shift_out, pool_out, conv_out = [], [], [], [], [], []
    for l in range(DEPTH):
        gn = W["norm_mix"][l][None]
        if l % 2 == 0:
            e = l // 2
            lam_init = 0.8 - 0.6 * math.exp(-0.3 * l)
            prm = _layer_params(W, e)
            (k, v, qkv, s_new, nkk, d, b, kt, r, vb, g, bonus) = _inproj(
                x, gn, W["w_in_bf"][e], shift_in[e], prm, tm=tm_in, stride=stride)
            lq = W["diff_lambda"][e]
            sub = W["diff_subln"][e][None]
            if prompt:
                oa = _pattn(qkv, lq, sub, lam_init, tq=512).reshape(N, A_WIDTH)
                streams = [_to_wkv_prompt(a) for a in (nkk, d, b, kt, r)]
                o, st = _wkv(*streams, _v_to_wkv_prompt(vb), wkv_in[e], tb=32)
                o = _o_from_wkv_prompt(o, Bn)
                st = _state_from_wkv_prompt(st, Bn)
            else:
                nt = T // Bn
                bm = lambda a: a.reshape(nt, Bn, -1).transpose(1, 0, 2)
                q_rep = jnp.tile(bm(qkv[0, :, :A_WIDTH].astype(F32)), (1, 2 * A_HEADS, 1))
                pad = ((0, 0), (0, NEW_KEY_ROWS - nt), (0, 0))
                oa = _sattn(attn_ctx["page_table"], q_rep, jnp.pad(bm(k[0]), pad), jnp.pad(bm(v[0]), pad),
                            lq, jnp.tile(sub, (1, A_HEADS)), attn_ctx["cache_k"][e], attn_ctx["cache_v"][e],
                            lam_init)
                oa = oa.transpose(1, 0, 2).reshape(N, A_WIDTH).astype(BF16)
                streams = [_to_wkv_sample(a[0], Bn) for a in (nkk, d, b, kt, r, vb)]
                o, st = _wkv(*streams, _state_to_wkv_sample(wkv_in[e]), tb=nt)
                o = _o_from_wkv_sample(o, Bn)
                st = _state_from_wkv_sample(st, Bn)
            x = _mixout(x.reshape(N, D_MODEL), oa, o, g.reshape(N, B_WIDTH), bonus.reshape(N, B_WIDTH),
                        W["rwkv_gn_w"][e][None], W["rwkv_gn_b"][e][None], prm["avg"], W["w_out_bf"][e],
                        tm=tm_flat).reshape(G, T, D_MODEL)
            k_rows.append(k)
            v_rows.append(v)
            wkv_out.append(st)
            shift_out.append(s_new)
        else:
            o_ = l // 2
            x, buf = _pool(x, gn, W["pool_w_bf"][o_], W["pool_scale"][o_][None], pool_in[o_],
                           tm=tm_pool, stride=stride, pos0=pos0)
            pool_out.append(buf)
        x, cbuf = _ffn(x, W["norm_ffn"][l][None], W["ffn_up_bf"][l], W["ffn_conv_w"][l],
                       W["ffn_conv_b"][l][None], W["ffn_down_bf"][l], conv_in[l], tm=tm_ffn, stride=stride)
        conv_out.append(cbuf)
    y = _final_norm(x.reshape(N, D_MODEL), W["norm_final"][None], tm=tm_flat).reshape(G, T, D_MODEL)
    return (y, jnp.stack(k_rows), jnp.stack(v_rows), jnp.stack(wkv_out), jnp.stack(shift_out),
            jnp.stack(pool_out), jnp.stack(conv_out))


def kernel(x_prompt, x_sample, cache_k, cache_v, state_wkv, state_shift, state_pool, state_ffn_conv,
           page_table, norm_mix, norm_ffn, norm_final, w_in, w_out, diff_lambda, diff_subln,
           rwkv_mu, rwkv_w0, rwkv_w2, rwkv_a0, rwkv_a2, rwkv_g2, rwkv_k_k, rwkv_k_a, rwkv_r_k,
           rwkv_gn_w, rwkv_gn_b, pool_w, pool_scale, ffn_up, ffn_conv_w, ffn_conv_b, ffn_down):
    W = dict(norm_mix=norm_mix, norm_ffn=norm_ffn, norm_final=norm_final,
             diff_lambda=diff_lambda, diff_subln=diff_subln, rwkv_mu=rwkv_mu, rwkv_w0=rwkv_w0,
             rwkv_w2=rwkv_w2, rwkv_a0=rwkv_a0, rwkv_a2=rwkv_a2, rwkv_g2=rwkv_g2, rwkv_k_k=rwkv_k_k,
             rwkv_k_a=rwkv_k_a, rwkv_r_k=rwkv_r_k, rwkv_gn_w=rwkv_gn_w, rwkv_gn_b=rwkv_gn_b,
             pool_scale=pool_scale, ffn_conv_w=ffn_conv_w, ffn_conv_b=ffn_conv_b,
             w_in_bf=w_in.astype(BF16), w_out_bf=w_out.astype(BF16), pool_w_bf=pool_w.astype(BF16),
             ffn_up_bf=ffn_up.astype(BF16), ffn_down_bf=ffn_down.astype(BF16))
    n_even, n_odd = state_wkv.shape[0], state_pool.shape[0]
    Bp, Sp, _ = x_prompt.shape
    Bs, Ts, _ = x_sample.shape
    n_pages = page_table.shape[1]
    past_len = n_pages * cache_k.shape[2]

    lanes_p = 2 * Bp * B_HEADS
    y_p, k_p, v_p, wkv_p, shift_p, pool_p, conv_p = _trunk(
        x_prompt, W, prompt=True,
        wkv_in=jnp.zeros((n_even, 1, B_HD // 2, B_HD, lanes_p), F32),
        shift_in=jnp.zeros((n_even, Bp, 1, B_COLS), F32),
        pool_in=jnp.zeros((n_odd, Bp, POOL_BUF, D_MODEL), F32),
        conv_in=jnp.zeros((DEPTH, Bp, CONV_W - 1, D_FF), F32), attn_ctx=None)
    k_p = k_p.reshape(n_even, Bp, Sp, A_HEADS, 2 * A_HD)
    v_p = v_p.reshape(n_even, Bp, Sp, A_HEADS, 2 * A_HD)
    shift_p = shift_p.reshape(n_even, Bp, B_COLS)

    tmaj = lambda a: jnp.swapaxes(a, -3, -2)
    flat = lambda a: a.reshape(a.shape[:-3] + (1, a.shape[-3] * a.shape[-2], a.shape[-1]))
    ctx = dict(batch=Bs, past_len=past_len, page_table=page_table,
               cache_k=cache_k.reshape(cache_k.shape[:3] + (A_WIDTH,)),
               cache_v=cache_v.reshape(cache_v.shape[:3] + (A_WIDTH,)))
    y_s, k_s, v_s, wkv_s, shift_s, pool_s, conv_s = _trunk(
        flat(tmaj(x_sample)), W, prompt=False, wkv_in=state_wkv,
        shift_in=state_shift[:, None], pool_in=flat(tmaj(state_pool)),
        conv_in=flat(tmaj(state_ffn_conv)), attn_ctx=ctx)
    unflat = lambda a, j: tmaj(a.reshape(a.shape[:-3] + (j, Bs, a.shape[-1])))
    y_s = unflat(y_s, Ts)
    k_s = unflat(k_s, Ts).reshape(n_even, Bs, Ts, A_HEADS, 2 * A_HD)
    v_s = unflat(v_s, Ts).reshape(n_even, Bs, Ts, A_HEADS, 2 * A_HD)
    shift_s = shift_s.reshape(n_even, Bs, B_COLS)
    pool_s = unflat(pool_s, POOL_BUF)
    conv_s = unflat(conv_s, CONV_W - 1)
    return (y_p, y_s, k_p, v_p, wkv_p, shift_p, pool_p, conv_p,
            k_s, v_s, wkv_s, shift_s, pool_s, conv_s)
```

```python
import functools
import math

import jax
import jax.numpy as jnp
from jax import lax
from jax.experimental import pallas as pl
from jax.experimental.pallas import tpu as pltpu

F32 = jnp.float32
BF16 = jnp.bfloat16

D_MODEL = 1024
DEPTH = 4
PAGE_SIZE = 128
A_HD = 64
A_HEADS = 4
A_WIDTH = 512
A_COLS = 3 * A_WIDTH
A_SCALE = A_HD ** -0.5
Q_SCALE = A_SCALE * math.log2(math.e)
B_HD = 64
B_WIDTH = 512
B_HEADS = 8
DECAY_RANK = 64
ICLR_RANK = 64
GATE_RANK = 128
B_COLS = 3 * B_WIDTH + DECAY_RANK + ICLR_RANK + GATE_RANK
IN_COLS = A_COLS + B_COLS
POOL_WINDOWS = (2, 4, 8, 16)
POOL_GW = D_MODEL // len(POOL_WINDOWS)
POOL_BUF = max(POOL_WINDOWS) - 1
D_FF = 2816
CONV_W = 3
NORM_EPS = 1e-6
SUBLN_EPS = 1e-5
GN_EPS = 64e-5

LANES = 128
SUBLANES = 8
VMEM_LIMIT = 56 * 1024 * 1024
FF_CHUNK = 256
PAGES_PER_STEP = 8
NEW_KEY_ROWS = 16
SROWS = 16
HEAD_LOG2 = 6


def _div_nonneg(x, n):
    if n == 1:
        return x
    if n & (n - 1) == 0:
        return lax.shift_right_logical(x, n.bit_length() - 1)
    return x // n


def _cparams(sem):
    return pltpu.CompilerParams(dimension_semantics=sem, vmem_limit_bytes=VMEM_LIMIT)


def _const_spec(shape):
    nd = len(shape)
    return pl.BlockSpec(shape, lambda *_: (0,) * nd, pipeline_mode=pl.Buffered(1))


def _rms(x, g, eps):
    return x * lax.rsqrt(jnp.mean(x * x, axis=-1, keepdims=True) + eps) * g


def _split_dot(x, m):
    hi = x.astype(BF16)
    lo = (x - hi.astype(F32)).astype(BF16)
    return (jnp.dot(hi, m, preferred_element_type=F32)
            + jnp.dot(lo, m, preferred_element_type=F32))


def _softplus(z):
    return jnp.maximum(z, 0.0) + jnp.log1p(jnp.exp(-jnp.abs(z)))


def _inproj_kernel(x_ref, gn_ref, win_ref, sprev_ref, mu_ref, w0_ref, w2p_ref, a0_ref, a2p_ref,
                   g2_ref, kk_ref, ka_ref, rk_ref, seg_ref,
                   k_ref, v_ref, qkv_ref, sout_ref, nkk_ref, d_ref, b_ref, kt_ref, r_ref,
                   vb_ref, g_ref, bonus_ref, ext_ref, *, tm, stride, halo):
    s = pl.program_id(1)
    ns = pl.num_programs(1)
    h = _rms(x_ref[...], gn_ref[...], NORM_EPS).astype(BF16)
    p = jnp.dot(h, win_ref[...], preferred_element_type=F32)
    k_ref[...] = p[:, A_WIDTH:2 * A_WIDTH]
    v_ref[...] = p[:, 2 * A_WIDTH:A_COLS]
    qkv_ref[:, :A_WIDTH] = (p[:, :A_WIDTH] * Q_SCALE).astype(BF16)
    qkv_ref[:, A_WIDTH:] = p[:, A_WIDTH:A_COLS].astype(BF16)
    pb = p[:, A_COLS:]

    @pl.when(s == 0)
    def _():
        ext_ref[halo - stride:halo, :] = sprev_ref[...]

    ext_ref[halo:halo + tm, :] = pb
    prev = ext_ref[halo - stride:halo - stride + tm, :]
    ext_ref[halo - stride:halo, :] = ext_ref[halo + tm - stride:halo + tm, :]

    @pl.when(s == ns - 1)
    def _():
        sout_ref[...] = ext_ref[halo + tm - stride:halo + tm, :]

    xm = pb + (prev - pb) * mu_ref[...]
    r = xm[:, :B_WIDTH]
    k = xm[:, B_WIDTH:2 * B_WIDTH]
    v = xm[:, 2 * B_WIDTH:3 * B_WIDTH]
    wa = xm[:, 3 * B_WIDTH:3 * B_WIDTH + DECAY_RANK + ICLR_RANK]
    gd = xm[:, 3 * B_WIDTH + DECAY_RANK + ICLR_RANK:]
    wlin = w0_ref[...] + jnp.dot(jnp.tanh(wa).astype(BF16), w2p_ref[...], preferred_element_type=F32)
    w = -_softplus(-wlin) - 0.5
    log_decay = -jnp.exp(w)
    a = jax.nn.sigmoid(a0_ref[...] + jnp.dot(wa.astype(BF16), a2p_ref[...], preferred_element_type=F32))
    g = jnp.dot(jax.nn.sigmoid(gd).astype(BF16), g2_ref[...], preferred_element_type=F32)
    kk = k * kk_ref[...]
    ss = _split_dot(kk * kk, seg_ref[...])
    kk = kk / jnp.maximum(jnp.sqrt(ss), 1e-12)
    kt = k * (1.0 + (a - 1.0) * ka_ref[...])
    bonus = _split_dot(r * kt * rk_ref[...], seg_ref[...]) * v
    nkk_ref[...] = -kk
    d_ref[...] = log_decay
    b_ref[...] = kk * a
    kt_ref[...] = kt
    r_ref[...] = r
    vb_ref[...] = v
    g_ref[...] = g
    bonus_ref[...] = bonus


def _inproj(x, gn, win, sprev, prm, *, tm, stride):
    G, T, _ = x.shape
    halo = max(SUBLANES, stride)
    ns = T // tm
    row = lambda w: pl.BlockSpec((None, tm, w), lambda g, s: (g, s, 0))
    stream = jax.ShapeDtypeStruct((G, T, B_WIDTH), F32)
    in_specs = [row(D_MODEL), _const_spec((1, D_MODEL)), _const_spec((D_MODEL, IN_COLS)),
                pl.BlockSpec((None, stride, B_COLS), lambda g, s: (g, 0, 0)),
                _const_spec((1, B_COLS)), _const_spec((1, B_WIDTH)), _const_spec((LANES, B_WIDTH)),
                _const_spec((1, B_WIDTH)), _const_spec((LANES, B_WIDTH)), _const_spec((GATE_RANK, B_WIDTH)),
                _const_spec((1, B_WIDTH)), _const_spec((1, B_WIDTH)), _const_spec((1, B_WIDTH)),
                _const_spec((B_WIDTH, B_WIDTH))]
    out_shape = [stream, stream, jax.ShapeDtypeStruct((G, T, A_COLS), BF16),
                 jax.ShapeDtypeStruct((G, stride, B_COLS), F32)] + [stream] * 8
    out_specs = [row(A_WIDTH), row(A_WIDTH), row(A_COLS),
                 pl.BlockSpec((None, stride, B_COLS), lambda g, s: (g, 0, 0))] + [row(B_WIDTH)] * 8
    return pl.pallas_call(
        functools.partial(_inproj_kernel, tm=tm, stride=stride, halo=halo),
        grid=(G, ns), in_specs=in_specs, out_specs=out_specs, out_shape=out_shape,
        scratch_shapes=[pltpu.VMEM((halo + tm, B_COLS), F32)],
        compiler_params=_cparams(("arbitrary", "arbitrary")), name="inproj",
    )(x, gn, win, sprev, prm["mu"], prm["w0"], prm["w2p"], prm["a0"], prm["a2p"], prm["g2"],
      prm["k_k"], prm["k_a"], prm["r_k"], prm["seg"])


def _lambda_full(lq, lam_init):
    s01 = jnp.sum(lq[0:1, :] * lq[1:2, :], axis=-1, keepdims=True)
    s23 = jnp.sum(lq[2:3, :] * lq[3:4, :], axis=-1, keepdims=True)
    return jnp.exp(s01) - jnp.exp(s23) + lam_init


def _pattn_kernel(q_ref, k_ref, v_ref, lq_ref, sub_ref, o_ref,
                  m0_ref, l0_ref, a0_ref, m1_ref, l1_ref, a1_ref, *, tq, lam_init):
    qi = pl.program_id(2)
    kj = pl.program_id(3)

    @pl.when(kj == 0)
    def _():
        for m_ref, l_ref, a_ref in ((m0_ref, l0_ref, a0_ref), (m1_ref, l1_ref, a1_ref)):
            m_ref[...] = jnp.full(m_ref.shape, -jnp.inf, F32)
            l_ref[...] = jnp.zeros(l_ref.shape, F32)
            a_ref[...] = jnp.zeros(a_ref.shape, F32)

    def accumulate(diagonal):
        q = q_ref[...]
        k = k_ref[...]
        v = v_ref[...]
        lane = lax.broadcasted_iota(jnp.int32, q.shape, 1)
        zero = jnp.zeros_like(q)
        if diagonal:
            keep = (lax.broadcasted_iota(jnp.int32, (tq, tq), 1)
                    <= lax.broadcasted_iota(jnp.int32, (tq, tq), 0))
        for c, (m_ref, l_ref, a_ref) in enumerate(((m0_ref, l0_ref, a0_ref), (m1_ref, l1_ref, a1_ref))):
            qc = jnp.where((lane >= c * A_HD) & (lane < (c + 1) * A_HD), q, zero)
            s = lax.dot_general(qc, k, (((1,), (1,)), ((), ())), preferred_element_type=F32)
            if diagonal:
                s = jnp.where(keep, s, -jnp.inf)
            m_old = m_ref[...]
            m_new = jnp.maximum(m_old, jnp.max(s, axis=-1, keepdims=True))
            alpha = jnp.exp2(m_old - m_new)
            p = jnp.exp2(s - m_new)
            l_ref[...] = alpha * l_ref[...] + jnp.sum(p, axis=-1, keepdims=True)
            a_ref[...] = alpha * a_ref[...] + jnp.dot(p.astype(BF16), v, preferred_element_type=F32)
            m_ref[...] = m_new

    @pl.when(kj < qi)
    def _():
        accumulate(False)

    @pl.when(kj == qi)
    def _():
        accumulate(True)
        lam = _lambda_full(lq_ref[...], lam_init)
        o = a0_ref[...] / l0_ref[...] - lam * (a1_ref[...] / l1_ref[...])
        o = _rms(o, sub_ref[...], SUBLN_EPS) * (1.0 - lam_init)
        o_ref[...] = o.astype(o_ref.dtype)


def _pattn(qkv, lq, sub, lam_init, *, tq):
    B, S, _ = qkv.shape
    nq = S // tq
    nh = A_HEADS
    q_spec = pl.BlockSpec((None, tq, LANES), lambda b, h, i, j: (b, i, h))
    k_spec = pl.BlockSpec((None, tq, LANES), lambda b, h, i, j: (b, jnp.minimum(i, j), nh + h))
    v_spec = pl.BlockSpec((None, tq, LANES), lambda b, h, i, j: (b, jnp.minimum(i, j), 2 * nh + h))
    return pl.pallas_call(
        functools.partial(_pattn_kernel, tq=tq, lam_init=lam_init),
        grid=(B, nh, nq, nq),
        in_specs=[q_spec, k_spec, v_spec,
                  pl.BlockSpec((4, A_HD), lambda b, h, i, j: (0, 0)),
                  pl.BlockSpec((1, LANES), lambda b, h, i, j: (0, 0))],
        out_specs=pl.BlockSpec((None, tq, LANES), lambda b, h, i, j: (b, i, h)),
        out_shape=jax.ShapeDtypeStruct((B, S, A_WIDTH), BF16),
        scratch_shapes=[pltpu.VMEM((tq, 1), F32), pltpu.VMEM((tq, 1), F32), pltpu.VMEM((tq, LANES), F32),
                        pltpu.VMEM((tq, 1), F32), pltpu.VMEM((tq, 1), F32), pltpu.VMEM((tq, LANES), F32)],
        compiler_params=_cparams(("arbitrary",) * 4), name="pattn",
    )(qkv, qkv, qkv, lq, sub)


def _sattn_kernel(pt_ref, q_ref, kn_ref, vn_ref, lq_ref, sub_ref, *rest, npg, nt, lam_init):
    k_refs = rest[:npg]
    v_refs = rest[npg:2 * npg]
    o_ref = rest[2 * npg]
    m_ref, l_ref, acc_ref = rest[2 * npg + 1:]
    j = pl.program_id(1)
    nj = pl.num_programs(1)

    @pl.when(j == 0)
    def _():
        m_ref[...] = jnp.full(m_ref.shape, -jnp.inf, F32)
        l_ref[...] = jnp.zeros(l_ref.shape, F32)
        acc_ref[...] = jnp.zeros(acc_ref.shape, F32)

    nt_dims = (((1,), (1,)), ((), ()))
    qs = [q_ref[h] for h in range(A_HEADS)]

    def head_rows(pg_ref, h):
        return pg_ref[pl.ds(h, PAGE_SIZE, stride=A_HEADS), :].astype(BF16)

    def update(s, pv):
        m_old = m_ref[...]
        m_new = jnp.maximum(m_old, jnp.max(s, axis=-1, keepdims=True))
        alpha = jnp.exp2(m_old - m_new)
        p = jnp.exp2(s - m_new)
        l_ref[...] = alpha * l_ref[...] + jnp.sum(p, axis=-1, keepdims=True)
        acc_ref[...] = alpha * acc_ref[...] + pv(p.astype(BF16))
        m_ref[...] = m_new

    s_heads = [jnp.concatenate(
        [lax.dot_general(qs[h], head_rows(kr, h), nt_dims, preferred_element_type=F32) for kr in k_refs],
        axis=1) for h in range(A_HEADS)]

    def pv_pages(p):
        outs = []
        for h in range(A_HEADS):
            ph = p[h * SROWS:(h + 1) * SROWS, :]
            tot = None
            for i, vr in enumerate(v_refs):
                part = jnp.dot(ph[:, i * PAGE_SIZE:(i + 1) * PAGE_SIZE], head_rows(vr, h),
                               preferred_element_type=F32)
                tot = part if tot is None else tot + part
            outs.append(tot)
        return jnp.concatenate(outs, axis=0)

    update(jnp.concatenate(s_heads, axis=0), pv_pages)

    @pl.when(j == nj - 1)
    def _():
        s_new = jnp.concatenate(
            [lax.dot_general(qs[h], kn_ref[h].astype(BF16), nt_dims, preferred_element_type=F32)
             for h in range(A_HEADS)], axis=0)
        key = lax.broadcasted_iota(jnp.int32, s_new.shape, 1)
        tok = lax.broadcasted_iota(jnp.int32, s_new.shape, 0) & (SROWS // 2 - 1)
        s_new = jnp.where((key <= tok) & (key < nt), s_new, -jnp.inf)

        def pv_new(p):
            return jnp.concatenate(
                [jnp.dot(p[h * SROWS:(h + 1) * SROWS, :], vn_ref[h].astype(BF16), preferred_element_type=F32)
                 for h in range(A_HEADS)], axis=0)

        update(s_new, pv_new)
        lam = _lambda_full(lq_ref[...], lam_init)
        half = SROWS // 2
        outs = []
        for h in range(A_HEADS):
            r0 = h * SROWS
            o = (acc_ref[r0:r0 + half, :] / l_ref[r0:r0 + half, :]
                 - lam * (acc_ref[r0 + half:r0 + SROWS, :] / l_ref[r0 + half:r0 + SROWS, :]))
            outs.append(_rms(o, sub_ref[...], SUBLN_EPS) * (1.0 - lam_init))
        o_ref[...] = jnp.concatenate(outs, axis=-1)


def _sattn(page_table, qh, k_new, v_new, lq, sub, kc, vc, e, nt, lam_init):
    Bn = qh.shape[0]
    n_pages = page_table.shape[1]
    npg = PAGES_PER_STEP
    nj = n_pages // npg
    page_rows = PAGE_SIZE * A_HEADS
    per_b = lambda r: pl.BlockSpec((None, A_HEADS, r, LANES), lambda b, j, pt: (b, 0, 0, 0))

    def page_spec(i):
        return pl.BlockSpec((None, None, page_rows, LANES), lambda b, j, pt: (e, pt[b, j * npg + i], 0, 0))

    in_specs = ([per_b(SROWS), per_b(NEW_KEY_ROWS), per_b(NEW_KEY_ROWS),
                 pl.BlockSpec((4, A_HD), lambda b, j, pt: (0, 0)),
                 pl.BlockSpec((1, LANES), lambda b, j, pt: (0, 0))]
                + [page_spec(i) for i in range(npg)] * 2)
    nr = A_HEADS * SROWS
    grid_spec = pltpu.PrefetchScalarGridSpec(
        num_scalar_prefetch=1, grid=(Bn, nj), in_specs=in_specs,
        out_specs=pl.BlockSpec((None, SROWS // 2, A_WIDTH), lambda b, j, pt: (b, 0, 0)),
        scratch_shapes=[pltpu.VMEM((nr, 1), F32), pltpu.VMEM((nr, 1), F32), pltpu.VMEM((nr, LANES), F32)])
    return pl.pallas_call(
        functools.partial(_sattn_kernel, npg=npg, nt=nt, lam_init=lam_init),
        grid_spec=grid_spec, out_shape=jax.ShapeDtypeStruct((Bn, SROWS // 2, A_WIDTH), F32),
        compiler_params=_cparams(("arbitrary", "arbitrary")), name="sattn",
    )(page_table, qh, k_new, v_new, lq, sub, *([kc] * npg), *([vc] * npg))


def _wkv_kernel(nkk_ref, d_ref, b_ref, kt_ref, r_ref, v_ref, s0_ref, o_ref, st_ref, s_scr, *, tb, nv):
    t0 = pl.program_id(1)

    @pl.when(t0 == 0)
    def _():
        s_scr[...] = s0_ref[...]

    def step(t, carry):
        nkk = nkk_ref[t]
        d = jnp.exp(d_ref[t])
        b = b_ref[t]
        kt = kt_ref[t]
        r = r_ref[t]
        dr = d * r
        br = jnp.sum(b * r, axis=0, keepdims=True)
        kr = jnp.sum(kt * r, axis=0, keepdims=True)

        def row(vr, c):
            S = s_scr[vr]
            sa = jnp.sum(S * nkk, axis=0, keepdims=True)
            qo = jnp.sum(S * dr, axis=0, keepdims=True)
            vv = v_ref[t, pl.ds(vr, 1), :]
            s_scr[vr] = S * d + sa * b + vv * kt
            o_ref[t, pl.ds(vr, 1), :] = qo + sa * br + vv * kr
            return c

        return lax.fori_loop(0, nv, row, carry, unroll=2)

    lax.fori_loop(0, tb, step, 0)
    st_ref[...] = s_scr[...]


def _wkv(nkk, d, b, kt, r, v, s0, *, tb):
    LG, T, _, _ = nkk.shape
    nv = v.shape[2]
    kspec = pl.BlockSpec((None, tb, B_HD, LANES), lambda g, t: (g, t, 0, 0))
    vspec = pl.BlockSpec((None, tb, nv, LANES), lambda g, t: (g, t, 0, 0))
    sspec = pl.BlockSpec((None, nv, B_HD, LANES), lambda g, t: (g, 0, 0, 0))
    return pl.pallas_call(
        functools.partial(_wkv_kernel, tb=tb, nv=nv),
        grid=(LG, T // tb), in_specs=[kspec] * 5 + [vspec, sspec],
        out_specs=[vspec, sspec],
        out_shape=[jax.ShapeDtypeStruct((LG, T, nv, LANES), F32),
                   jax.ShapeDtypeStruct((LG, nv, B_HD, LANES), F32)],
        scratch_shapes=[pltpu.VMEM((nv, B_HD, LANES), F32)],
        compiler_params=_cparams(("arbitrary", "arbitrary")), name="wkv",
    )(nkk, d, b, kt, r, v, s0)


WKV_CHUNK = 64
WKV_GROUP = 4


def _split3_dot(m, x):
    x1 = x.astype(BF16)
    r1 = x - x1.astype(F32)
    x2 = r1.astype(BF16)
    x3 = (r1 - x2.astype(F32)).astype(BF16)
    return (jnp.dot(m, x1, preferred_element_type=F32) + jnp.dot(m, x2, preferred_element_type=F32)
            + jnp.dot(m, x3, preferred_element_type=F32))


def _wkv_chunk_math(a, ld, b, kt, r, v, s):
    L, HW = WKV_CHUNK, WKV_GROUP * B_HD
    nt_dims = (((1,), (1,)), ((), ()))
    tn_dims = (((0,), (0,)), ((), ()))
    row = lax.broadcasted_iota(jnp.int32, (L, HW), 0)
    col = lax.broadcasted_iota(jnp.int32, (L, HW), 1) & (B_HD - 1)
    strict = col < row
    incl = col <= row
    blk = lax.shift_right_logical
    bd_mask = (lax.shift_right_logical(lax.broadcasted_iota(jnp.int32, (HW, HW), 0), HEAD_LOG2)
               == lax.shift_right_logical(lax.broadcasted_iota(jnp.int32, (HW, HW), 1), HEAD_LOG2))

    def bd(y):
        y = y.astype(BF16)
        return jnp.where(bd_mask, jnp.concatenate([y] * WKV_GROUP, axis=0), jnp.zeros((), BF16))

    def hprod(x, y):
        return jnp.dot(x.astype(BF16), bd(y), preferred_element_type=F32)

    def each(f, *lists):
        return [f(*xs) for xs in zip(*lists)]

    def dotg(dims):
        return lambda x, y: lax.dot_general(x, y, dims, preferred_element_type=F32)

    tri = (lax.broadcasted_iota(jnp.int32, (L, L), 1) <= lax.broadcasted_iota(jnp.int32, (L, L), 0)).astype(BF16)
    logp = each(lambda x: _split3_dot(tri, x), ld)
    ep = each(jnp.exp, logp)
    p_last = each(lambda x: x[L - 1:L, :], ep)
    en = each(lambda x: jnp.exp(-x), logp)
    at = each(lambda x, lp, l_: x * jnp.exp(lp - l_), a, logp, ld)
    rt = each(jnp.multiply, r, ep)
    bt = each(jnp.multiply, b, en)
    ktt = each(jnp.multiply, kt, en)
    bh = each(lambda x, pp: (x * pp).astype(BF16), bt, p_last)
    kh = each(lambda x, pp: (x * pp).astype(BF16), ktt, p_last)

    lhs = each(lambda x, y: jnp.concatenate([x, y], axis=0).astype(BF16), at, rt)
    xb = each(lambda x, y: dotg(nt_dims)(x, bd(y)), lhs, bt)
    xk = each(lambda x, y: dotg(nt_dims)(x, bd(y)), lhs, ktt)
    n = each(lambda x: jnp.where(strict, x[:L], 0.0), xb)
    arb = each(lambda x: jnp.where(incl, x[L:], 0.0), xb)
    aak = each(lambda x: jnp.where(strict, x[:L], 0.0), xk)
    ark = each(lambda x: jnp.where(incl, x[L:], 0.0), xk)

    eye = jnp.where(col == row, 1.0, 0.0)
    tm = each(lambda x: eye + jnp.where(blk(row, 1) == blk(col, 1), x, 0.0), n)
    for lg in range(1, HEAD_LOG2):
        sel = (blk(row, lg + 1) == blk(col, lg + 1)) & (blk(row, lg) != blk(col, lg))
        w = each(lambda t_, x: hprod(t_, jnp.where(sel, x, 0.0)), tm, n)
        tm = each(lambda t_, w_: t_ + hprod(w_, t_), tm, w)

    rloc = each(hprod, aak, v)
    uloc = each(hprod, tm, rloc)
    ah = each(hprod, tm, at)
    ro = each(lambda x, y, z: (x + hprod(y, z)).astype(BF16), rt, arb, ah)
    oc = each(lambda x, u, y, vv: hprod(x, u) + hprod(y, vv), arb, uloc, ark, v)
    mpp = each(lambda x, y: jnp.where(bd_mask, dotg(tn_dims)(x.astype(BF16), y), 0.0).astype(BF16), ah, bh)
    sc = each(lambda u, y, vv, z: jnp.where(
        bd_mask, dotg(tn_dims)(u.astype(BF16), y) + dotg(tn_dims)(vv.astype(BF16), z), 0.0), uloc, bh, v, kh)

    s_hi = each(lambda x: x.astype(BF16), s)
    s_lo = each(lambda x, h: (x - h.astype(F32)).astype(BF16), s, s_hi)
    o = each(lambda q, h, l_, c_: dotg(nt_dims)(q, h) + dotg(nt_dims)(q, l_) + c_, ro, s_hi, s_lo, oc)
    s_new = each(lambda x, pp, h, l_, m_, c_: (x * pp + jnp.dot(h, m_, preferred_element_type=F32)
                                               + jnp.dot(l_, m_, preferred_element_type=F32) + c_),
                 s, p_last, s_hi, s_lo, mpp, sc)
    return o, s_new


def _wkv_chunk_kernel(a_ref, ld_ref, b_ref, kt_ref, r_ref, v_ref, o_ref, st_ref, s_scr, *, nb):
    c = pl.program_id(1)
    nc = pl.num_programs(1)
    HW = WKV_GROUP * B_HD

    @pl.when(c == 0)
    def _():
        s_scr[...] = jnp.zeros(s_scr.shape, F32)

    groups = [(bb, g) for bb in range(nb) for g in range(B_WIDTH // HW)]
    sls = [(bb, slice(None), slice(g * HW, (g + 1) * HW)) for bb, g in groups]
    o, s_new = _wkv_chunk_math(*[[ref[sl] for sl in sls] for ref in (a_ref, ld_ref, b_ref, kt_ref, r_ref, v_ref)],
                               [s_scr[bb, g] for bb, g in groups])
    for sl, (bb, g), o_i, s_i in zip(sls, groups, o, s_new):
        o_ref[sl] = o_i
        s_scr[bb, g] = s_i

    @pl.when(c == nc - 1)
    def _():
        st_ref[...] = s_scr[...]


def _wkv_chunked(a, ld, b, kt, r, v, *, nb=4):
    Bn, T, _ = a.shape
    L, HW = WKV_CHUNK, WKV_GROUP * B_HD
    ng = B_WIDTH // HW
    spec = pl.BlockSpec((nb, L, B_WIDTH), lambda bb, c: (bb, c, 0))
    st_spec = pl.BlockSpec((nb, ng, HW, HW), lambda bb, c: (bb, 0, 0, 0))
    o, st = pl.pallas_call(
        functools.partial(_wkv_chunk_kernel, nb=nb),
        grid=(Bn // nb, T // L), in_specs=[spec] * 6, out_specs=[spec, st_spec],
        out_shape=[jax.ShapeDtypeStruct((Bn, T, B_WIDTH), F32), jax.ShapeDtypeStruct((Bn, ng, HW, HW), F32)],
        scratch_shapes=[pltpu.VMEM((nb, ng, HW, HW), F32)],
        compiler_params=_cparams(("arbitrary",) * 2), name="wkv_chunked",
    )(a, ld, b, kt, r, v)
    st = st.reshape(Bn, ng, WKV_GROUP, B_HD, WKV_GROUP, B_HD)
    st = jnp.stack([st[:, :, i, :, i, :] for i in range(WKV_GROUP)], axis=2)
    return o, st.reshape(Bn, B_HEADS, B_HD, B_HD)


def _mixout_kernel(x_ref, oa_ref, o_ref, g_ref, bonus_ref, gnw_ref, gnb_ref, avg_ref, wout_ref, xo_ref):
    o = o_ref[...]
    mu = _split_dot(o, avg_ref[...])
    dlt = o - mu
    var = _split_dot(dlt * dlt, avg_ref[...])
    on = dlt * lax.rsqrt(var + GN_EPS) * gnw_ref[...] + gnb_ref[...]
    ob = ((on + bonus_ref[...]) * g_ref[...]).astype(BF16)
    y = (jnp.dot(oa_ref[...], wout_ref[0:A_WIDTH, :], preferred_element_type=F32)
         + jnp.dot(ob, wout_ref[A_WIDTH:, :], preferred_element_type=F32))
    xo_ref[...] = x_ref[...] + y


def _mixout(x, oa, o, g, bonus, gnw, gnb, avg, wout, *, tm):
    N = x.shape[0]
    row = lambda w: pl.BlockSpec((tm, w), lambda i: (i, 0))
    return pl.pallas_call(
        _mixout_kernel, grid=(N // tm,),
        in_specs=[row(D_MODEL), row(A_WIDTH), row(B_WIDTH), row(B_WIDTH), row(B_WIDTH),
                  _const_spec((1, B_WIDTH)), _const_spec((1, B_WIDTH)),
                  _const_spec((B_WIDTH, B_WIDTH)), _const_spec((D_MODEL, D_MODEL))],
        out_specs=row(D_MODEL), out_shape=jax.ShapeDtypeStruct((N, D_MODEL), F32),
        compiler_params=_cparams(("arbitrary",)), name="mixout",
    )(x, oa, o, g, bonus, gnw, gnb, avg, wout)


def _pool_kernel(x_ref, gn_ref, pw_ref, ps_ref, pprev_ref, xo_ref, pout_ref, ext_ref,
                 *, tm, stride, halo, pos0):
    s = pl.program_id(1)
    ns = pl.num_programs(1)
    keep = POOL_BUF * stride
    x = x_ref[...]
    h = _rms(x, gn_ref[...], NORM_EPS)

    @pl.when(s == 0)
    def _():
        ext_ref[halo - keep:halo, :] = pprev_ref[...]

    ext_ref[halo:halo + tm, :] = h
    t = _div_nonneg(s * tm + lax.broadcasted_iota(jnp.int32, (tm, 1), 0), stride)
    ys = []
    for gi, win in enumerate(POOL_WINDOWS):
        c0, c1 = gi * POOL_GW, (gi + 1) * POOL_GW
        cur = ext_ref[halo:halo + tm, c0:c1]
        wsum = cur
        for jj in range(1, win):
            wsum = wsum + ext_ref[halo - jj * stride:halo - jj * stride + tm, c0:c1]
        cnt = jnp.minimum(pos0 + t + 1, win).astype(F32)
        m = wsum / cnt - cur
        ys.append(jnp.dot(m.astype(BF16), pw_ref[gi], preferred_element_type=F32))
    xo_ref[...] = x + jnp.concatenate(ys, axis=-1) * ps_ref[...]

    @pl.when(s == ns - 1)
    def _():
        pout_ref[...] = ext_ref[halo + tm - keep:halo + tm, :]

    if tm >= keep:
        ext_ref[halo - keep:halo, :] = ext_ref[halo + tm - keep:halo + tm, :]


def _pool(x, gn, pw, ps, pprev, *, tm, stride, pos0):
    G, T, _ = x.shape
    halo = 2 * SUBLANES * stride
    keep = POOL_BUF * stride
    assert T // tm == 1 or tm >= keep
    row = pl.BlockSpec((None, tm, D_MODEL), lambda g, s: (g, s, 0))
    st = pl.BlockSpec((None, keep, D_MODEL), lambda g, s: (g, 0, 0))
    return pl.pallas_call(
        functools.partial(_pool_kernel, tm=tm, stride=stride, halo=halo, pos0=pos0),
        grid=(G, T // tm),
        in_specs=[row, _const_spec((1, D_MODEL)), _const_spec((len(POOL_WINDOWS), POOL_GW, POOL_GW)),
                  _const_spec((1, D_MODEL)), st],
        out_specs=[row, st],
        out_shape=[jax.ShapeDtypeStruct((G, T, D_MODEL), F32), jax.ShapeDtypeStruct((G, keep, D_MODEL), F32)],
        scratch_shapes=[pltpu.VMEM((halo + tm, D_MODEL), F32)],
        compiler_params=_cparams(("arbitrary", "arbitrary")), name="pool",
    )(x, gn, pw, ps, pprev)


def _ffn_kernel(x_ref, gn_ref, wup_ref, cw_ref, cb_ref, wdn_ref, cprev_ref, xo_ref, cout_ref, ext_ref,
                *, tm, stride, halo):
    s = pl.program_id(1)
    ns = pl.num_programs(1)
    keep = (CONV_W - 1) * stride
    x = x_ref[...]
    h = _rms(x, gn_ref[...], NORM_EPS).astype(BF16)

    @pl.when(s == 0)
    def _():
        ext_ref[halo - keep:halo, :] = cprev_ref[...]

    acc = x
    for c in range(D_FF // FF_CHUNK):
        c0, c1 = c * FF_CHUNK, (c + 1) * FF_CHUNK
        gate = jnp.dot(h, wup_ref[:, c0:c1], preferred_element_type=F32)
        val = jnp.dot(h, wup_ref[:, D_FF + c0:D_FF + c1], preferred_element_type=F32)
        ext_ref[halo:halo + tm, c0:c1] = gate
        cc = cb_ref[:, c0:c1]
        for jj in range(CONV_W - 1):
            back = (CONV_W - 1 - jj) * stride
            cc = cc + ext_ref[halo - back:halo - back + tm, c0:c1] * cw_ref[jj:jj + 1, c0:c1]
        cc = cc + gate * cw_ref[CONV_W - 1:CONV_W, c0:c1]
        act = (0.5 * cc * (1.0 + lax.erf(cc * (2.0 ** -0.5))) * val).astype(BF16)
        acc = acc + jnp.dot(act, wdn_ref[c0:c1, :], preferred_element_type=F32)
    xo_ref[...] = acc

    @pl.when(s == ns - 1)
    def _():
        cout_ref[...] = ext_ref[halo + tm - keep:halo + tm, :]

    ext_ref[halo - keep:halo, :] = ext_ref[halo + tm - keep:halo + tm, :]


def _ffn(x, gn, wup, cw, cb, wdn, cprev, *, tm, stride):
    G, T, _ = x.shape
    keep = (CONV_W - 1) * stride
    halo = max(SUBLANES, keep)
    assert tm >= keep
    row = pl.BlockSpec((None, tm, D_MODEL), lambda g, s: (g, s, 0))
    st = pl.BlockSpec((None, keep, D_FF), lambda g, s: (g, 0, 0))
    return pl.pallas_call(
        functools.partial(_ffn_kernel, tm=tm, stride=stride, halo=halo),
        grid=(G, T // tm),
        in_specs=[row, _const_spec((1, D_MODEL)), _const_spec((D_MODEL, 2 * D_FF)),
                  _const_spec((CONV_W, D_FF)), _const_spec((1, D_FF)), _const_spec((D_FF, D_MODEL)), st],
        out_specs=[row, st],
        out_shape=[jax.ShapeDtypeStruct((G, T, D_MODEL), F32), jax.ShapeDtypeStruct((G, keep, D_FF), F32)],
        scratch_shapes=[pltpu.VMEM((halo + tm, D_FF), F32)],
        compiler_params=_cparams(("arbitrary", "arbitrary")), name="ffn",
    )(x, gn, wup, cw, cb, wdn, cprev)


def _final_norm_kernel(x_ref, g_ref, o_ref):
    o_ref[...] = _rms(x_ref[...], g_ref[...], NORM_EPS)


def _final_norm(x, g, *, tm):
    N = x.shape[0]
    row = pl.BlockSpec((tm, D_MODEL), lambda i: (i, 0))
    return pl.pallas_call(
        _final_norm_kernel, grid=(N // tm,), in_specs=[row, _const_spec((1, D_MODEL))], out_specs=row,
        out_shape=jax.ShapeDtypeStruct((N, D_MODEL), F32),
        compiler_params=_cparams(("arbitrary",)), name="final_norm",
    )(x, g)


def _to_wkv_sample(a, Bn):
    T = a.shape[0] // Bn
    a = a.reshape(T, Bn, B_HEADS, B_HD).transpose(0, 3, 1, 2).reshape(T, B_HD, Bn * B_HEADS // LANES, LANES)
    return a.transpose(2, 0, 1, 3)


def _o_from_wkv_sample(o, Bn):
    LG, T = o.shape[0], o.shape[1]
    o = o.transpose(1, 2, 0, 3).reshape(T, B_HD, Bn, B_HEADS).transpose(0, 2, 3, 1)
    return o.reshape(T * Bn, B_WIDTH)


def _state_to_wkv_sample(st):
    Bn = st.shape[0]
    st = st.transpose(2, 3, 0, 1).reshape(B_HD, B_HD, Bn * B_HEADS // LANES, LANES)
    return st.transpose(2, 0, 1, 3)


def _state_from_wkv_sample(st, Bn):
    st = st.transpose(1, 2, 0, 3).reshape(B_HD, B_HD, Bn, B_HEADS)
    return st.transpose(2, 3, 0, 1)


def _layer_params(W, e):
    z = jnp.zeros((DECAY_RANK, B_WIDTH), F32)
    head = jnp.arange(B_WIDTH) // B_HD
    same = (head[:, None] == head[None, :])
    return {
        "mu": W["rwkv_mu"][e][None], "w0": W["rwkv_w0"][e][None],
        "w2p": jnp.concatenate([W["rwkv_w2"][e], z], axis=0).astype(BF16),
        "a0": W["rwkv_a0"][e][None],
        "a2p": jnp.concatenate([z, W["rwkv_a2"][e]], axis=0).astype(BF16),
        "g2": W["rwkv_g2"][e].astype(BF16),
        "k_k": W["rwkv_k_k"][e][None], "k_a": W["rwkv_k_a"][e][None],
        "r_k": W["rwkv_r_k"][e].reshape(1, B_WIDTH),
        "seg": same.astype(BF16), "avg": (same.astype(F32) / B_HD).astype(BF16),
    }


def _trunk(x, W, *, prompt, wkv_in, shift_in, pool_in, conv_in, attn_ctx):
    G, T, _ = x.shape
    if prompt:
        Bn, stride, tm_in, tm_ffn, tm_pool, pos0 = G, 1, 256, 512, 512, 0
    else:
        Bn, stride, pos0 = attn_ctx["batch"], attn_ctx["batch"], attn_ctx["past_len"]
        tm_in = tm_ffn = tm_pool = T
    N = G * T
    tm_flat = min(N, 512)
    k_rows, v_rows, wkv_out, shift_out, pool_out, conv_out = [], [], [], [], [], []
    for l in range(DEPTH):
        gn = W["norm_mix"][l][None]
        if l % 2 == 0:
            e = l // 2
            lam_init = 0.8 - 0.6 * math.exp(-0.3 * l)
            prm = _layer_params(W, e)
            (k, v, qkv, s_new, nkk, d, b, kt, r, vb, g, bonus) = _inproj(
                x, gn, W["w_in_bf"][e], shift_in[e], prm, tm=tm_in, stride=stride)
            lq = W["diff_lambda"][e]
            sub = W["diff_subln"][e][None]
            if prompt:
                oa = _pattn(qkv, lq, sub, lam_init, tq=512).reshape(N, A_WIDTH)
                o, st = _wkv_chunked(nkk, d, b, kt, r, vb)
                o = o.reshape(N, B_WIDTH)
            else:
                nt = T // Bn
                bh = lambda a: a.reshape(nt, Bn, A_HEADS, 2 * A_HD).transpose(1, 2, 0, 3)
                padr = lambda a, rows: jnp.pad(a, ((0, 0), (0, 0), (0, rows - a.shape[2]), (0, 0)))
                q8 = padr(bh(qkv[0, :, :A_WIDTH]), SROWS // 2)
                comp = (jnp.arange(2 * A_HD) // A_HD)
                qh = jnp.concatenate([jnp.where(comp == c, q8, jnp.zeros((), BF16)) for c in range(2)], axis=2)
                oa = _sattn(attn_ctx["page_table"], qh, padr(bh(k[0]), NEW_KEY_ROWS), padr(bh(v[0]), NEW_KEY_ROWS),
                            lq, sub, attn_ctx["cache_k"], attn_ctx["cache_v"], e, nt, lam_init)
                oa = oa[:, :nt].transpose(1, 0, 2).reshape(N, A_WIDTH).astype(BF16)
                streams = [_to_wkv_sample(a[0], Bn) for a in (nkk, d, b, kt, r, vb)]
                o, st = _wkv(*streams, _state_to_wkv_sample(wkv_in[e]), tb=nt)
                o = _o_from_wkv_sample(o, Bn)
                st = _state_from_wkv_sample(st, Bn)
            x = _mixout(x.reshape(N, D_MODEL), oa, o, g.reshape(N, B_WIDTH), bonus.reshape(N, B_WIDTH),
                        W["rwkv_gn_w"][e][None], W["rwkv_gn_b"][e][None], prm["avg"], W["w_out_bf"][e],
                        tm=tm_flat).reshape(G, T, D_MODEL)
            k_rows.append(k)
            v_rows.append(v)
            wkv_out.append(st)
            shift_out.append(s_new)
        else:
            o_ = l // 2
            x, buf = _pool(x, gn, W["pool_w_bf"][o_], W["pool_scale"][o_][None], pool_in[o_],
                           tm=tm_pool, stride=stride, pos0=pos0)
            pool_out.append(buf)
        x, cbuf = _ffn(x, W["norm_ffn"][l][None], W["ffn_up_bf"][l], W["ffn_conv_w"][l],
                       W["ffn_conv_b"][l][None], W["ffn_down_bf"][l], conv_in[l], tm=tm_ffn, stride=stride)
        conv_out.append(cbuf)
    y = _final_norm(x.reshape(N, D_MODEL), W["norm_final"][None], tm=tm_flat).reshape(G, T, D_MODEL)
    return (y, jnp.stack(k_rows), jnp.stack(v_rows), jnp.stack(wkv_out), jnp.stack(shift_out),
            jnp.stack(pool_out), jnp.stack(conv_out))


def kernel(x_prompt, x_sample, cache_k, cache_v, state_wkv, state_shift, state_pool, state_ffn_conv,
           page_table, norm_mix, norm_ffn, norm_final, w_in, w_out, diff_lambda, diff_subln,
           rwkv_mu, rwkv_w0, rwkv_w2, rwkv_a0, rwkv_a2, rwkv_g2, rwkv_k_k, rwkv_k_a, rwkv_r_k,
           rwkv_gn_w, rwkv_gn_b, pool_w, pool_scale, ffn_up, ffn_conv_w, ffn_conv_b, ffn_down):
    W = dict(norm_mix=norm_mix, norm_ffn=norm_ffn, norm_final=norm_final,
             diff_lambda=diff_lambda, diff_subln=diff_subln, rwkv_mu=rwkv_mu, rwkv_w0=rwkv_w0,
             rwkv_w2=rwkv_w2, rwkv_a0=rwkv_a0, rwkv_a2=rwkv_a2, rwkv_g2=rwkv_g2, rwkv_k_k=rwkv_k_k,
             rwkv_k_a=rwkv_k_a, rwkv_r_k=rwkv_r_k, rwkv_gn_w=rwkv_gn_w, rwkv_gn_b=rwkv_gn_b,
             pool_scale=pool_scale, ffn_conv_w=ffn_conv_w, ffn_conv_b=ffn_conv_b,
             w_in_bf=w_in.astype(BF16), w_out_bf=w_out.astype(BF16), pool_w_bf=pool_w.astype(BF16),
             ffn_up_bf=ffn_up.astype(BF16), ffn_down_bf=ffn_down.astype(BF16))
    n_even, n_odd = state_wkv.shape[0], state_pool.shape[0]
    Bp, Sp, _ = x_prompt.shape
    Bs, Ts, _ = x_sample.shape
    n_pages = page_table.shape[1]
    past_len = n_pages * cache_k.shape[2]

    y_p, k_p, v_p, wkv_p, shift_p, pool_p, conv_p = _trunk(
        x_prompt, W, prompt=True, wkv_in=None,
        shift_in=jnp.zeros((n_even, Bp, 1, B_COLS), F32),
        pool_in=jnp.zeros((n_odd, Bp, POOL_BUF, D_MODEL), F32),
        conv_in=jnp.zeros((DEPTH, Bp, CONV_W - 1, D_FF), F32), attn_ctx=None)
    k_p = k_p.reshape(n_even, Bp, Sp, A_HEADS, 2 * A_HD)
    v_p = v_p.reshape(n_even, Bp, Sp, A_HEADS, 2 * A_HD)
    shift_p = shift_p.reshape(n_even, Bp, B_COLS)

    tmaj = lambda a: jnp.swapaxes(a, -3, -2)
    flat = lambda a: a.reshape(a.shape[:-3] + (1, a.shape[-3] * a.shape[-2], a.shape[-1]))
    ctx = dict(batch=Bs, past_len=past_len, page_table=page_table,
               cache_k=cache_k.reshape(cache_k.shape[:2] + (PAGE_SIZE * A_HEADS, 2 * A_HD)),
               cache_v=cache_v.reshape(cache_v.shape[:2] + (PAGE_SIZE * A_HEADS, 2 * A_HD)))
    y_s, k_s, v_s, wkv_s, shift_s, pool_s, conv_s = _trunk(
        flat(tmaj(x_sample)), W, prompt=False, wkv_in=state_wkv,
        shift_in=state_shift[:, None], pool_in=flat(tmaj(state_pool)),
        conv_in=flat(tmaj(state_ffn_conv)), attn_ctx=ctx)
    unflat = lambda a, j: tmaj(a.reshape(a.shape[:-3] + (j, Bs, a.shape[-1])))
    y_s = unflat(y_s, Ts)
    k_s = unflat(k_s, Ts).reshape(n_even, Bs, Ts, A_HEADS, 2 * A_HD)
    v_s = unflat(v_s, Ts).reshape(n_even, Bs, Ts, A_HEADS, 2 * A_HD)
    shift_s = shift_s.reshape(n_even, Bs, B_COLS)
    pool_s = unflat(pool_s, POOL_BUF)
    conv_s = unflat(conv_s, CONV_W - 1)
    return (y_p, y_s, k_p, v_p, wkv_p, shift_p, pool_p, conv_p,
            k_s, v_s, wkv_s, shift_s, pool_s, conv_s)
```

```python
import functools
import math

import jax
import jax.numpy as jnp
from jax import lax
from jax.experimental import pallas as pl
from jax.experimental.pallas import tpu as pltpu

F32 = jnp.float32
BF16 = jnp.bfloat16

D_MODEL = 1024
DEPTH = 4
PAGE_SIZE = 128
A_HD = 64
A_HEADS = 4
A_WIDTH = 512
A_COLS = 3 * A_WIDTH
A_SCALE = A_HD ** -0.5
Q_SCALE = A_SCALE * math.log2(math.e)
B_HD = 64
B_WIDTH = 512
B_HEADS = 8
DECAY_RANK = 64
ICLR_RANK = 64
GATE_RANK = 128
B_COLS = 3 * B_WIDTH + DECAY_RANK + ICLR_RANK + GATE_RANK
IN_COLS = A_COLS + B_COLS
POOL_WINDOWS = (2, 4, 8, 16)
POOL_GW = D_MODEL // len(POOL_WINDOWS)
POOL_BUF = max(POOL_WINDOWS) - 1
D_FF = 2816
CONV_W = 3
NORM_EPS = 1e-6
SUBLN_EPS = 1e-5
GN_EPS = 64e-5

LANES = 128
SUBLANES = 8
VMEM_LIMIT = 56 * 1024 * 1024
FF_CHUNK = 256
PAGES_PER_STEP = 8
NEW_KEY_ROWS = 16
SROWS = 16
HEAD_LOG2 = 6


def _div_nonneg(x, n):
    if n == 1:
        return x
    if n & (n - 1) == 0:
        return lax.shift_right_logical(x, n.bit_length() - 1)
    return x // n


def _cparams(sem):
    return pltpu.CompilerParams(dimension_semantics=sem, vmem_limit_bytes=VMEM_LIMIT)


def _const_spec(shape):
    nd = len(shape)
    return pl.BlockSpec(shape, lambda *_: (0,) * nd, pipeline_mode=pl.Buffered(1))


def _rms(x, g, eps):
    return x * lax.rsqrt(jnp.mean(x * x, axis=-1, keepdims=True) + eps) * g


def _split_dot(x, m):
    hi = x.astype(BF16)
    lo = (x - hi.astype(F32)).astype(BF16)
    return (jnp.dot(hi, m, preferred_element_type=F32)
            + jnp.dot(lo, m, preferred_element_type=F32))


def _softplus(z):
    return jnp.maximum(z, 0.0) + jnp.log1p(jnp.exp(-jnp.abs(z)))


def _inproj_kernel(x_ref, gn_ref, win_ref, sprev_ref, mu_ref, w0_ref, w2p_ref, a0_ref, a2p_ref,
                   g2_ref, kk_ref, ka_ref, rk_ref, seg_ref,
                   k_ref, v_ref, qkv_ref, sout_ref, nkk_ref, d_ref, b_ref, kt_ref, r_ref,
                   vb_ref, g_ref, bonus_ref, ext_ref, *, tm, stride, halo):
    s = pl.program_id(1)
    ns = pl.num_programs(1)
    h = _rms(x_ref[...], gn_ref[...], NORM_EPS).astype(BF16)
    p = jnp.dot(h, win_ref[...], preferred_element_type=F32)
    for hh in range(A_HEADS):
        c0 = A_WIDTH + hh * 2 * A_HD
        k_ref[pl.ds(hh, tm, stride=A_HEADS), :] = p[:, c0:c0 + 2 * A_HD]
        v_ref[pl.ds(hh, tm, stride=A_HEADS), :] = p[:, A_WIDTH + c0:A_WIDTH + c0 + 2 * A_HD]
    qkv_ref[:, :A_WIDTH] = (p[:, :A_WIDTH] * Q_SCALE).astype(BF16)
    qkv_ref[:, A_WIDTH:] = p[:, A_WIDTH:A_COLS].astype(BF16)
    pb = p[:, A_COLS:]

    @pl.when(s == 0)
    def _():
        ext_ref[halo - stride:halo, :] = sprev_ref[...]

    ext_ref[halo:halo + tm, :] = pb
    prev = ext_ref[halo - stride:halo - stride + tm, :]
    ext_ref[halo - stride:halo, :] = ext_ref[halo + tm - stride:halo + tm, :]

    @pl.when(s == ns - 1)
    def _():
        sout_ref[...] = ext_ref[halo + tm - stride:halo + tm, :]

    xm = pb + (prev - pb) * mu_ref[...]
    r = xm[:, :B_WIDTH]
    k = xm[:, B_WIDTH:2 * B_WIDTH]
    v = xm[:, 2 * B_WIDTH:3 * B_WIDTH]
    wa = xm[:, 3 * B_WIDTH:3 * B_WIDTH + DECAY_RANK + ICLR_RANK]
    gd = xm[:, 3 * B_WIDTH + DECAY_RANK + ICLR_RANK:]
    wlin = w0_ref[...] + jnp.dot(jnp.tanh(wa).astype(BF16), w2p_ref[...], preferred_element_type=F32)
    w = -_softplus(-wlin) - 0.5
    log_decay = -jnp.exp(w)
    a = jax.nn.sigmoid(a0_ref[...] + jnp.dot(wa.astype(BF16), a2p_ref[...], preferred_element_type=F32))
    g = jnp.dot(jax.nn.sigmoid(gd).astype(BF16), g2_ref[...], preferred_element_type=F32)
    kk = k * kk_ref[...]
    ss = _split_dot(kk * kk, seg_ref[...])
    kk = kk / jnp.maximum(jnp.sqrt(ss), 1e-12)
    kt = k * (1.0 + (a - 1.0) * ka_ref[...])
    bonus = _split_dot(r * kt * rk_ref[...], seg_ref[...]) * v
    nkk_ref[...] = -kk
    d_ref[...] = log_decay
    b_ref[...] = kk * a
    kt_ref[...] = kt
    r_ref[...] = r
    vb_ref[...] = v
    g_ref[...] = g
    bonus_ref[...] = bonus


def _inproj(x, gn, win, sprev, prm, *, tm, stride):
    G, T, _ = x.shape
    halo = max(SUBLANES, stride)
    ns = T // tm
    row = lambda w: pl.BlockSpec((None, tm, w), lambda g, s: (g, s, 0))
    stream = jax.ShapeDtypeStruct((G, T, B_WIDTH), F32)
    in_specs = [row(D_MODEL), _const_spec((1, D_MODEL)), _const_spec((D_MODEL, IN_COLS)),
                pl.BlockSpec((None, stride, B_COLS), lambda g, s: (g, 0, 0)),
                _const_spec((1, B_COLS)), _const_spec((1, B_WIDTH)), _const_spec((LANES, B_WIDTH)),
                _const_spec((1, B_WIDTH)), _const_spec((LANES, B_WIDTH)), _const_spec((GATE_RANK, B_WIDTH)),
                _const_spec((1, B_WIDTH)), _const_spec((1, B_WIDTH)), _const_spec((1, B_WIDTH)),
                _const_spec((B_WIDTH, B_WIDTH))]
    kv_shape = jax.ShapeDtypeStruct((G, T * A_HEADS, 2 * A_HD), F32)
    kv_spec = pl.BlockSpec((None, tm * A_HEADS, 2 * A_HD), lambda g, s: (g, s, 0))
    out_shape = [kv_shape, kv_shape, jax.ShapeDtypeStruct((G, T, A_COLS), BF16),
                 jax.ShapeDtypeStruct((G, stride, B_COLS), F32)] + [stream] * 8
    out_specs = [kv_spec, kv_spec, row(A_COLS),
                 pl.BlockSpec((None, stride, B_COLS), lambda g, s: (g, 0, 0))] + [row(B_WIDTH)] * 8
    return pl.pallas_call(
        functools.partial(_inproj_kernel, tm=tm, stride=stride, halo=halo),
        grid=(G, ns), in_specs=in_specs, out_specs=out_specs, out_shape=out_shape,
        scratch_shapes=[pltpu.VMEM((halo + tm, B_COLS), F32)],
        compiler_params=_cparams(("arbitrary", "arbitrary")), name="inproj",
    )(x, gn, win, sprev, prm["mu"], prm["w0"], prm["w2p"], prm["a0"], prm["a2p"], prm["g2"],
      prm["k_k"], prm["k_a"], prm["r_k"], prm["seg"])


def _lambda_full(lq, lam_init):
    s01 = jnp.sum(lq[0:1, :] * lq[1:2, :], axis=-1, keepdims=True)
    s23 = jnp.sum(lq[2:3, :] * lq[3:4, :], axis=-1, keepdims=True)
    return jnp.exp(s01) - jnp.exp(s23) + lam_init


def _pattn_kernel(qi_ref, kj_ref, q_ref, k_ref, v_ref, lq_ref, sub_ref, o_ref,
                  m0_ref, l0_ref, a0_ref, m1_ref, l1_ref, a1_ref, *, tq, lam_init):
    qi = qi_ref[pl.program_id(2)]
    kj = kj_ref[pl.program_id(2)]

    @pl.when(kj == 0)
    def _():
        for m_ref, l_ref, a_ref in ((m0_ref, l0_ref, a0_ref), (m1_ref, l1_ref, a1_ref)):
            m_ref[...] = jnp.full(m_ref.shape, -jnp.inf, F32)
            l_ref[...] = jnp.zeros(l_ref.shape, F32)
            a_ref[...] = jnp.zeros(a_ref.shape, F32)

    def accumulate(diagonal):
        q = q_ref[...]
        k = k_ref[...]
        v = v_ref[...]
        lane = lax.broadcasted_iota(jnp.int32, q.shape, 1)
        zero = jnp.zeros_like(q)
        if diagonal:
            keep = (lax.broadcasted_iota(jnp.int32, (tq, tq), 1)
                    <= lax.broadcasted_iota(jnp.int32, (tq, tq), 0))
        refs = ((m0_ref, l0_ref, a0_ref), (m1_ref, l1_ref, a1_ref))
        comps = range(2)
        qc = [jnp.where((lane >= c * A_HD) & (lane < (c + 1) * A_HD), q, zero) for c in comps]
        s = [lax.dot_general(qc[c], k, (((1,), (1,)), ((), ())), preferred_element_type=F32) for c in comps]
        if diagonal:
            s = [jnp.where(keep, s[c], -jnp.inf) for c in comps]
        m_old = [refs[c][0][...] for c in comps]
        m_new = [jnp.maximum(m_old[c], jnp.max(s[c], axis=-1, keepdims=True)) for c in comps]
        alpha = [jnp.exp2(m_old[c] - m_new[c]) for c in comps]
        p = [jnp.exp2(s[c] - pltpu.repeat(m_new[c], tq // LANES, axis=1)) for c in comps]
        for c in comps:
            m_ref, l_ref, a_ref = refs[c]
            l_ref[...] = alpha[c] * l_ref[...] + jnp.sum(p[c], axis=-1, keepdims=True)
            a_ref[...] = alpha[c] * a_ref[...] + jnp.dot(p[c].astype(BF16), v, preferred_element_type=F32)
            m_ref[...] = m_new[c]

    @pl.when(kj < qi)
    def _():
        accumulate(False)

    @pl.when(kj == qi)
    def _():
        accumulate(True)
        lam = _lambda_full(lq_ref[...], lam_init)
        o = a0_ref[...] / l0_ref[...] - lam * (a1_ref[...] / l1_ref[...])
        o = _rms(o, sub_ref[...], SUBLN_EPS) * (1.0 - lam_init)
        o_ref[...] = o.astype(o_ref.dtype)


def _pattn(qkv, lq, sub, lam_init, *, tq):
    B, S, _ = qkv.shape
    nq = S // tq
    nh = A_HEADS
    pairs = [(i, j) for i in range(nq) for j in range(i + 1)]
    qi_tab = jnp.asarray([p[0] for p in pairs], jnp.int32)
    kj_tab = jnp.asarray([p[1] for p in pairs], jnp.int32)
    q_spec = pl.BlockSpec((None, tq, LANES), lambda b, h, n, qt, kt: (b, qt[n], h))
    k_spec = pl.BlockSpec((None, tq, LANES), lambda b, h, n, qt, kt: (b, kt[n], nh + h))
    v_spec = pl.BlockSpec((None, tq, LANES), lambda b, h, n, qt, kt: (b, kt[n], 2 * nh + h))
    grid_spec = pltpu.PrefetchScalarGridSpec(
        num_scalar_prefetch=2, grid=(B, nh, len(pairs)),
        in_specs=[q_spec, k_spec, v_spec,
                  pl.BlockSpec((4, A_HD), lambda b, h, n, qt, kt: (0, 0)),
                  pl.BlockSpec((1, LANES), lambda b, h, n, qt, kt: (0, 0))],
        out_specs=pl.BlockSpec((None, tq, LANES), lambda b, h, n, qt, kt: (b, qt[n], h)),
        scratch_shapes=[pltpu.VMEM((tq, LANES), F32)] * 6)
    return pl.pallas_call(
        functools.partial(_pattn_kernel, tq=tq, lam_init=lam_init),
        grid_spec=grid_spec, out_shape=jax.ShapeDtypeStruct((B, S, A_WIDTH), BF16),
        compiler_params=_cparams(("arbitrary",) * 3), name="pattn",
    )(qi_tab, kj_tab, qkv, qkv, qkv, lq, sub)


def _sattn_kernel(pt_ref, q_ref, kn_ref, vn_ref, lq_ref, sub_ref, *rest, npg, nt, lam_init):
    k_refs = rest[:npg]
    v_refs = rest[npg:2 * npg]
    o_ref = rest[2 * npg]
    m_ref, l_ref, acc_ref = rest[2 * npg + 1:]
    j = pl.program_id(1)
    nj = pl.num_programs(1)

    @pl.when(j == 0)
    def _():
        m_ref[...] = jnp.full(m_ref.shape, -jnp.inf, F32)
        l_ref[...] = jnp.zeros(l_ref.shape, F32)
        acc_ref[...] = jnp.zeros(acc_ref.shape, F32)

    nt_dims = (((1,), (1,)), ((), ()))
    qs = [q_ref[h] for h in range(A_HEADS)]

    def head_rows(pg_ref, h):
        return pg_ref[pl.ds(h, PAGE_SIZE, stride=A_HEADS), :].astype(BF16)

    def update(s, pv):
        m_old = m_ref[...]
        m_new = jnp.maximum(m_old, jnp.max(s, axis=-1, keepdims=True))
        alpha = jnp.exp2(m_old - m_new)
        p = jnp.exp2(s - (pltpu.repeat(m_new, s.shape[1] // LANES, axis=1) if s.shape[1] > LANES
                          else m_new[:, :s.shape[1]]))
        l_ref[...] = alpha * l_ref[...] + jnp.sum(p, axis=-1, keepdims=True)
        acc_ref[...] = alpha * acc_ref[...] + pv(p.astype(BF16))
        m_ref[...] = m_new

    s_heads = [jnp.concatenate(
        [lax.dot_general(qs[h], head_rows(kr, h), nt_dims, preferred_element_type=F32) for kr in k_refs],
        axis=1) for h in range(A_HEADS)]

    def pv_pages(p):
        outs = []
        for h in range(A_HEADS):
            ph = p[h * SROWS:(h + 1) * SROWS, :]
            tot = None
            for i, vr in enumerate(v_refs):
                part = jnp.dot(ph[:, i * PAGE_SIZE:(i + 1) * PAGE_SIZE], head_rows(vr, h),
                               preferred_element_type=F32)
                tot = part if tot is None else tot + part
            outs.append(tot)
        return jnp.concatenate(outs, axis=0)

    update(jnp.concatenate(s_heads, axis=0), pv_pages)

    @pl.when(j == nj - 1)
    def _():
        s_new = jnp.concatenate(
            [lax.dot_general(qs[h], kn_ref[h].astype(BF16), nt_dims, preferred_element_type=F32)
             for h in range(A_HEADS)], axis=0)
        key = lax.broadcasted_iota(jnp.int32, s_new.shape, 1)
        tok = lax.broadcasted_iota(jnp.int32, s_new.shape, 0) & (SROWS // 2 - 1)
        s_new = jnp.where((key <= tok) & (key < nt), s_new, -jnp.inf)

        def pv_new(p):
            return jnp.concatenate(
                [jnp.dot(p[h * SROWS:(h + 1) * SROWS, :], vn_ref[h].astype(BF16), preferred_element_type=F32)
                 for h in range(A_HEADS)], axis=0)

        update(s_new, pv_new)
        lam = _lambda_full(lq_ref[...], lam_init)
        half = SROWS // 2
        outs = []
        for h in range(A_HEADS):
            r0 = h * SROWS
            o = (acc_ref[r0:r0 + half, :] / l_ref[r0:r0 + half, :]
                 - lam * (acc_ref[r0 + half:r0 + SROWS, :] / l_ref[r0 + half:r0 + SROWS, :]))
            outs.append(_rms(o, sub_ref[...], SUBLN_EPS) * (1.0 - lam_init))
        o_ref[...] = jnp.concatenate(outs, axis=-1)


def _sattn(page_table, qh, k_new, v_new, lq, sub, kc, vc, e, nt, lam_init):
    Bn = qh.shape[0]
    n_pages = page_table.shape[1]
    npg = PAGES_PER_STEP
    nj = n_pages // npg
    page_rows = PAGE_SIZE * A_HEADS
    per_b = lambda r: pl.BlockSpec((None, A_HEADS, r, LANES), lambda b, j, pt: (b, 0, 0, 0))

    def page_spec(i):
        return pl.BlockSpec((None, None, page_rows, LANES), lambda b, j, pt: (e, pt[b, j * npg + i], 0, 0))

    in_specs = ([per_b(SROWS), per_b(NEW_KEY_ROWS), per_b(NEW_KEY_ROWS),
                 pl.BlockSpec((4, A_HD), lambda b, j, pt: (0, 0)),
                 pl.BlockSpec((1, LANES), lambda b, j, pt: (0, 0))]
                + [page_spec(i) for i in range(npg)] * 2)
    nr = A_HEADS * SROWS
    grid_spec = pltpu.PrefetchScalarGridSpec(
        num_scalar_prefetch=1, grid=(Bn, nj), in_specs=in_specs,
        out_specs=pl.BlockSpec((None, SROWS // 2, A_WIDTH), lambda b, j, pt: (b, 0, 0)),
        scratch_shapes=[pltpu.VMEM((nr, LANES), F32)] * 3)
    return pl.pallas_call(
        functools.partial(_sattn_kernel, npg=npg, nt=nt, lam_init=lam_init),
        grid_spec=grid_spec, out_shape=jax.ShapeDtypeStruct((Bn, SROWS // 2, A_WIDTH), F32),
        compiler_params=_cparams(("arbitrary", "arbitrary")), name="sattn",
    )(page_table, qh, k_new, v_new, lq, sub, *([kc] * npg), *([vc] * npg))


def _wkv_kernel(nkk_ref, d_ref, b_ref, kt_ref, r_ref, v_ref, s0_ref, o_ref, st_ref, s_scr, *, tb, nv):
    t0 = pl.program_id(1)

    @pl.when(t0 == 0)
    def _():
        s_scr[...] = s0_ref[...]

    def step(t, carry):
        nkk = nkk_ref[t]
        d = jnp.exp(d_ref[t])
        b = b_ref[t]
        kt = kt_ref[t]
        r = r_ref[t]
        dr = d * r
        br = jnp.sum(b * r, axis=0, keepdims=True)
        kr = jnp.sum(kt * r, axis=0, keepdims=True)

        def row(vr, c):
            S = s_scr[vr]
            sa = jnp.sum(S * nkk, axis=0, keepdims=True)
            qo = jnp.sum(S * dr, axis=0, keepdims=True)
            vv = v_ref[t, pl.ds(vr, 1), :]
            s_scr[vr] = S * d + sa * b + vv * kt
            o_ref[t, pl.ds(vr, 1), :] = qo + sa * br + vv * kr
            return c

        return lax.fori_loop(0, nv, row, carry, unroll=2)

    lax.fori_loop(0, tb, step, 0)
    st_ref[...] = s_scr[...]


def _wkv(nkk, d, b, kt, r, v, s0, *, tb):
    LG, T, _, _ = nkk.shape
    nv = v.shape[2]
    kspec = pl.BlockSpec((None, tb, B_HD, LANES), lambda g, t: (g, t, 0, 0))
    vspec = pl.BlockSpec((None, tb, nv, LANES), lambda g, t: (g, t, 0, 0))
    sspec = pl.BlockSpec((None, nv, B_HD, LANES), lambda g, t: (g, 0, 0, 0))
    return pl.pallas_call(
        functools.partial(_wkv_kernel, tb=tb, nv=nv),
        grid=(LG, T // tb), in_specs=[kspec] * 5 + [vspec, sspec],
        out_specs=[vspec, sspec],
        out_shape=[jax.ShapeDtypeStruct((LG, T, nv, LANES), F32),
                   jax.ShapeDtypeStruct((LG, nv, B_HD, LANES), F32)],
        scratch_shapes=[pltpu.VMEM((nv, B_HD, LANES), F32)],
        compiler_params=_cparams(("arbitrary", "arbitrary")), name="wkv",
    )(nkk, d, b, kt, r, v, s0)


WKV_CHUNK = 64
WKV_GROUP = 4


def _split3_dot(m, x):
    x1 = x.astype(BF16)
    r1 = x - x1.astype(F32)
    x2 = r1.astype(BF16)
    x3 = (r1 - x2.astype(F32)).astype(BF16)
    return (jnp.dot(m, x1, preferred_element_type=F32) + jnp.dot(m, x2, preferred_element_type=F32)
            + jnp.dot(m, x3, preferred_element_type=F32))


def _wkv_chunk_math(a, ld, b, kt, r, v, s):
    L, HW = WKV_CHUNK, WKV_GROUP * B_HD
    nt_dims = (((1,), (1,)), ((), ()))
    tn_dims = (((0,), (0,)), ((), ()))
    row = lax.broadcasted_iota(jnp.int32, (L, HW), 0)
    col = lax.broadcasted_iota(jnp.int32, (L, HW), 1) & (B_HD - 1)
    strict = col < row
    incl = col <= row
    blk = lax.shift_right_logical
    bd_mask = (lax.shift_right_logical(lax.broadcasted_iota(jnp.int32, (HW, HW), 0), HEAD_LOG2)
               == lax.shift_right_logical(lax.broadcasted_iota(jnp.int32, (HW, HW), 1), HEAD_LOG2))

    def bd(y):
        y = y.astype(BF16)
        return jnp.where(bd_mask, jnp.concatenate([y] * WKV_GROUP, axis=0), jnp.zeros((), BF16))

    def hprod(x, y):
        return jnp.dot(x.astype(BF16), bd(y), preferred_element_type=F32)

    def each(f, *lists):
        return [f(*xs) for xs in zip(*lists)]

    def dotg(dims):
        return lambda x, y: lax.dot_general(x, y, dims, preferred_element_type=F32)

    tri = (lax.broadcasted_iota(jnp.int32, (L, L), 1) <= lax.broadcasted_iota(jnp.int32, (L, L), 0)).astype(BF16)
    logp = each(lambda x: _split3_dot(tri, x), ld)
    ep = each(jnp.exp, logp)
    p_last = each(lambda x: x[L - 1:L, :], ep)
    en = each(lambda x: jnp.exp(-x), logp)
    at = each(lambda x, lp, l_: x * jnp.exp(lp - l_), a, logp, ld)
    rt = each(jnp.multiply, r, ep)
    bt = each(jnp.multiply, b, en)
    ktt = each(jnp.multiply, kt, en)
    bh = each(lambda x, pp: (x * pp).astype(BF16), bt, p_last)
    kh = each(lambda x, pp: (x * pp).astype(BF16), ktt, p_last)

    lhs = each(lambda x, y: jnp.concatenate([x, y], axis=0).astype(BF16), at, rt)
    xb = each(lambda x, y: dotg(nt_dims)(x, bd(y)), lhs, bt)
    xk = each(lambda x, y: dotg(nt_dims)(x, bd(y)), lhs, ktt)
    n = each(lambda x: jnp.where(strict, x[:L], 0.0), xb)
    arb = each(lambda x: jnp.where(incl, x[L:], 0.0), xb)
    aak = each(lambda x: jnp.where(strict, x[:L], 0.0), xk)
    ark = each(lambda x: jnp.where(incl, x[L:], 0.0), xk)

    eye = jnp.where(col == row, 1.0, 0.0)
    tm = each(lambda x: eye + jnp.where(blk(row, 1) == blk(col, 1), x, 0.0), n)
    for lg in range(1, HEAD_LOG2):
        sel = (blk(row, lg + 1) == blk(col, lg + 1)) & (blk(row, lg) != blk(col, lg))
        w = each(lambda t_, x: hprod(t_, jnp.where(sel, x, 0.0)), tm, n)
        tm = each(lambda t_, w_: t_ + hprod(w_, t_), tm, w)

    rloc = each(hprod, aak, v)
    uloc = each(hprod, tm, rloc)
    ah = each(hprod, tm, at)
    ro = each(lambda x, y, z: (x + hprod(y, z)).astype(BF16), rt, arb, ah)
    oc = each(lambda x, u, y, vv: hprod(x, u) + hprod(y, vv), arb, uloc, ark, v)
    mpp = each(lambda x, y: jnp.where(bd_mask, dotg(tn_dims)(x.astype(BF16), y), 0.0).astype(BF16), ah, bh)
    sc = each(lambda u, y, vv, z: jnp.where(
        bd_mask, dotg(tn_dims)(u.astype(BF16), y) + dotg(tn_dims)(vv.astype(BF16), z), 0.0), uloc, bh, v, kh)

    s_hi = each(lambda x: x.astype(BF16), s)
    s_lo = each(lambda x, h: (x - h.astype(F32)).astype(BF16), s, s_hi)
    o = each(lambda q, h, l_, c_: dotg(nt_dims)(q, h) + dotg(nt_dims)(q, l_) + c_, ro, s_hi, s_lo, oc)
    s_new = each(lambda x, pp, h, l_, m_, c_: (x * pp + jnp.dot(h, m_, preferred_element_type=F32)
                                               + jnp.dot(l_, m_, preferred_element_type=F32) + c_),
                 s, p_last, s_hi, s_lo, mpp, sc)
    return o, s_new


def _wkv_chunk_kernel(a_ref, ld_ref, b_ref, kt_ref, r_ref, v_ref, o_ref, st_ref, s_scr, *, nb):
    c = pl.program_id(1)
    nc = pl.num_programs(1)
    HW = WKV_GROUP * B_HD

    @pl.when(c == 0)
    def _():
        s_scr[...] = jnp.zeros(s_scr.shape, F32)

    groups = [(bb, g) for bb in range(nb) for g in range(B_WIDTH // HW)]
    sls = [(bb, slice(None), slice(g * HW, (g + 1) * HW)) for bb, g in groups]
    o, s_new = _wkv_chunk_math(*[[ref[sl] for sl in sls] for ref in (a_ref, ld_ref, b_ref, kt_ref, r_ref, v_ref)],
                               [s_scr[bb, g] for bb, g in groups])
    for sl, (bb, g), o_i, s_i in zip(sls, groups, o, s_new):
        o_ref[sl] = o_i
        s_scr[bb, g] = s_i

    @pl.when(c == nc - 1)
    def _():
        st_ref[...] = s_scr[...]


def _wkv_chunked(a, ld, b, kt, r, v, *, nb=4):
    Bn, T, _ = a.shape
    L, HW = WKV_CHUNK, WKV_GROUP * B_HD
    ng = B_WIDTH // HW
    spec = pl.BlockSpec((nb, L, B_WIDTH), lambda bb, c: (bb, c, 0))
    st_spec = pl.BlockSpec((nb, ng, HW, HW), lambda bb, c: (bb, 0, 0, 0))
    o, st = pl.pallas_call(
        functools.partial(_wkv_chunk_kernel, nb=nb),
        grid=(Bn // nb, T // L), in_specs=[spec] * 6, out_specs=[spec, st_spec],
        out_shape=[jax.ShapeDtypeStruct((Bn, T, B_WIDTH), F32), jax.ShapeDtypeStruct((Bn, ng, HW, HW), F32)],
        scratch_shapes=[pltpu.VMEM((nb, ng, HW, HW), F32)],
        compiler_params=_cparams(("arbitrary",) * 2), name="wkv_chunked",
    )(a, ld, b, kt, r, v)
    st = st.reshape(Bn, ng, WKV_GROUP, B_HD, WKV_GROUP, B_HD)
    st = jnp.stack([st[:, :, i, :, i, :] for i in range(WKV_GROUP)], axis=2)
    return o, st.reshape(Bn, B_HEADS, B_HD, B_HD)


def _mixout_kernel(x_ref, oa_ref, o_ref, g_ref, bonus_ref, gnw_ref, gnb_ref, avg_ref, wout_ref, xo_ref):
    o = o_ref[...]
    mu = _split_dot(o, avg_ref[...])
    dlt = o - mu
    var = _split_dot(dlt * dlt, avg_ref[...])
    on = dlt * lax.rsqrt(var + GN_EPS) * gnw_ref[...] + gnb_ref[...]
    ob = ((on + bonus_ref[...]) * g_ref[...]).astype(BF16)
    y = (jnp.dot(oa_ref[...], wout_ref[0:A_WIDTH, :], preferred_element_type=F32)
         + jnp.dot(ob, wout_ref[A_WIDTH:, :], preferred_element_type=F32))
    xo_ref[...] = x_ref[...] + y


def _mixout(x, oa, o, g, bonus, gnw, gnb, avg, wout, *, tm):
    N = x.shape[0]
    row = lambda w: pl.BlockSpec((tm, w), lambda i: (i, 0))
    return pl.pallas_call(
        _mixout_kernel, grid=(N // tm,),
        in_specs=[row(D_MODEL), row(A_WIDTH), row(B_WIDTH), row(B_WIDTH), row(B_WIDTH),
                  _const_spec((1, B_WIDTH)), _const_spec((1, B_WIDTH)),
                  _const_spec((B_WIDTH, B_WIDTH)), _const_spec((D_MODEL, D_MODEL))],
        out_specs=row(D_MODEL), out_shape=jax.ShapeDtypeStruct((N, D_MODEL), F32),
        compiler_params=_cparams(("arbitrary",)), name="mixout",
    )(x, oa, o, g, bonus, gnw, gnb, avg, wout)


def _pool_kernel(x_ref, gn_ref, pw_ref, ps_ref, pprev_ref, xo_ref, pout_ref, ext_ref,
                 *, tm, stride, halo, pos0):
    s = pl.program_id(1)
    ns = pl.num_programs(1)
    keep = POOL_BUF * stride
    x = x_ref[...]
    h = _rms(x, gn_ref[...], NORM_EPS)

    @pl.when(s == 0)
    def _():
        ext_ref[halo - keep:halo, :] = pprev_ref[...]

    ext_ref[halo:halo + tm, :] = h
    t = _div_nonneg(s * tm + lax.broadcasted_iota(jnp.int32, (tm, 1), 0), stride)
    ys = []
    for gi, win in enumerate(POOL_WINDOWS):
        c0, c1 = gi * POOL_GW, (gi + 1) * POOL_GW
        cur = ext_ref[halo:halo + tm, c0:c1]
        wsum = cur
        for jj in range(1, win):
            wsum = wsum + ext_ref[halo - jj * stride:halo - jj * stride + tm, c0:c1]
        cnt = jnp.minimum(pos0 + t + 1, win).astype(F32)
        m = wsum / cnt - cur
        ys.append(jnp.dot(m.astype(BF16), pw_ref[gi], preferred_element_type=F32))
    xo_ref[...] = x + jnp.concatenate(ys, axis=-1) * ps_ref[...]

    @pl.when(s == ns - 1)
    def _():
        pout_ref[...] = ext_ref[halo + tm - keep:halo + tm, :]

    if tm >= keep:
        ext_ref[halo - keep:halo, :] = ext_ref[halo + tm - keep:halo + tm, :]


def _pool(x, gn, pw, ps, pprev, *, tm, stride, pos0):
    G, T, _ = x.shape
    halo = 2 * SUBLANES * stride
    keep = POOL_BUF * stride
    assert T // tm == 1 or tm >= keep
    row = pl.BlockSpec((None, tm, D_MODEL), lambda g, s: (g, s, 0))
    st = pl.BlockSpec((None, keep, D_MODEL), lambda g, s: (g, 0, 0))
    return pl.pallas_call(
        functools.partial(_pool_kernel, tm=tm, stride=stride, halo=halo, pos0=pos0),
        grid=(G, T // tm),
        in_specs=[row, _const_spec((1, D_MODEL)), _const_spec((len(POOL_WINDOWS), POOL_GW, POOL_GW)),
                  _const_spec((1, D_MODEL)), st],
        out_specs=[row, st],
        out_shape=[jax.ShapeDtypeStruct((G, T, D_MODEL), F32), jax.ShapeDtypeStruct((G, keep, D_MODEL), F32)],
        scratch_shapes=[pltpu.VMEM((halo + tm, D_MODEL), F32)],
        compiler_params=_cparams(("arbitrary", "arbitrary")), name="pool",
    )(x, gn, pw, ps, pprev)


def _ffn_kernel(x_ref, gn_ref, wup_ref, cw_ref, cb_ref, wdn_ref, cprev_ref, xo_ref, cout_ref, ext_ref,
                *, tm, stride, halo):
    s = pl.program_id(1)
    ns = pl.num_programs(1)
    keep = (CONV_W - 1) * stride
    x = x_ref[...]
    h = _rms(x, gn_ref[...], NORM_EPS).astype(BF16)

    @pl.when(s == 0)
    def _():
        ext_ref[halo - keep:halo, :] = cprev_ref[...]

    def up(c):
        c0, c1 = c * FF_CHUNK, (c + 1) * FF_CHUNK
        gate = jnp.dot(h, wup_ref[:, c0:c1], preferred_element_type=F32)
        val = jnp.dot(h, wup_ref[:, D_FF + c0:D_FF + c1], preferred_element_type=F32)
        ext_ref[halo:halo + tm, c0:c1] = gate
        return gate, val

    def down(c, gate, val):
        c0, c1 = c * FF_CHUNK, (c + 1) * FF_CHUNK
        cc = cb_ref[:, c0:c1]
        for jj in range(CONV_W - 1):
            back = (CONV_W - 1 - jj) * stride
            cc = cc + ext_ref[halo - back:halo - back + tm, c0:c1] * cw_ref[jj:jj + 1, c0:c1]
        cc = cc + gate * cw_ref[CONV_W - 1:CONV_W, c0:c1]
        act = (0.5 * cc * (1.0 + lax.erf(cc * (2.0 ** -0.5))) * val).astype(BF16)
        return jnp.dot(act, wdn_ref[c0:c1, :], preferred_element_type=F32)

    nchunk = D_FF // FF_CHUNK
    acc = x
    pending = up(0)
    for c in range(nchunk):
        nxt = up(c + 1) if c + 1 < nchunk else None
        acc = acc + down(c, *pending)
        pending = nxt
    xo_ref[...] = acc

    @pl.when(s == ns - 1)
    def _():
        cout_ref[...] = ext_ref[halo + tm - keep:halo + tm, :]

    ext_ref[halo - keep:halo, :] = ext_ref[halo + tm - keep:halo + tm, :]


def _ffn(x, gn, wup, cw, cb, wdn, cprev, *, tm, stride):
    G, T, _ = x.shape
    keep = (CONV_W - 1) * stride
    halo = max(SUBLANES, keep)
    assert tm >= keep
    row = pl.BlockSpec((None, tm, D_MODEL), lambda g, s: (g, s, 0))
    st = pl.BlockSpec((None, keep, D_FF), lambda g, s: (g, 0, 0))
    return pl.pallas_call(
        functools.partial(_ffn_kernel, tm=tm, stride=stride, halo=halo),
        grid=(G, T // tm),
        in_specs=[row, _const_spec((1, D_MODEL)), _const_spec((D_MODEL, 2 * D_FF)),
                  _const_spec((CONV_W, D_FF)), _const_spec((1, D_FF)), _const_spec((D_FF, D_MODEL)), st],
        out_specs=[row, st],
        out_shape=[jax.ShapeDtypeStruct((G, T, D_MODEL), F32), jax.ShapeDtypeStruct((G, keep, D_FF), F32)],
        scratch_shapes=[pltpu.VMEM((halo + tm, D_FF), F32)],
        compiler_params=_cparams(("arbitrary", "arbitrary")), name="ffn",
    )(x, gn, wup, cw, cb, wdn, cprev)


def _final_norm_kernel(x_ref, g_ref, o_ref):
    o_ref[...] = _rms(x_ref[...], g_ref[...], NORM_EPS)


def _final_norm(x, g, *, tm):
    N = x.shape[0]
    row = pl.BlockSpec((tm, D_MODEL), lambda i: (i, 0))
    return pl.pallas_call(
        _final_norm_kernel, grid=(N // tm,), in_specs=[row, _const_spec((1, D_MODEL))], out_specs=row,
        out_shape=jax.ShapeDtypeStruct((N, D_MODEL), F32),
        compiler_params=_cparams(("arbitrary",)), name="final_norm",
    )(x, g)


def _to_wkv_sample(a, Bn):
    T = a.shape[0] // Bn
    a = a.reshape(T, Bn, B_HEADS, B_HD).transpose(0, 3, 1, 2).reshape(T, B_HD, Bn * B_HEADS // LANES, LANES)
    return a.transpose(2, 0, 1, 3)


def _o_from_wkv_sample(o, Bn):
    LG, T = o.shape[0], o.shape[1]
    o = o.transpose(1, 2, 0, 3).reshape(T, B_HD, Bn, B_HEADS).transpose(0, 2, 3, 1)
    return o.reshape(T * Bn, B_WIDTH)


def _state_to_wkv_sample(st):
    Bn = st.shape[0]
    st = st.transpose(2, 3, 0, 1).reshape(B_HD, B_HD, Bn * B_HEADS // LANES, LANES)
    return st.transpose(2, 0, 1, 3)


def _state_from_wkv_sample(st, Bn):
    st = st.transpose(1, 2, 0, 3).reshape(B_HD, B_HD, Bn, B_HEADS)
    return st.transpose(2, 3, 0, 1)


def _layer_params(W, e):
    z = jnp.zeros((DECAY_RANK, B_WIDTH), F32)
    head = jnp.arange(B_WIDTH) // B_HD
    same = (head[:, None] == head[None, :])
    return {
        "mu": W["rwkv_mu"][e][None], "w0": W["rwkv_w0"][e][None],
        "w2p": jnp.concatenate([W["rwkv_w2"][e], z], axis=0).astype(BF16),
        "a0": W["rwkv_a0"][e][None],
        "a2p": jnp.concatenate([z, W["rwkv_a2"][e]], axis=0).astype(BF16),
        "g2": W["rwkv_g2"][e].astype(BF16),
        "k_k": W["rwkv_k_k"][e][None], "k_a": W["rwkv_k_a"][e][None],
        "r_k": W["rwkv_r_k"][e].reshape(1, B_WIDTH),
        "seg": same.astype(BF16), "avg": (same.astype(F32) / B_HD).astype(BF16),
    }


def _trunk(x, W, *, prompt, wkv_in, shift_in, pool_in, conv_in, attn_ctx):
    G, T, _ = x.shape
    if prompt:
        Bn, stride, tm_in, tm_ffn, tm_pool, pos0 = G, 1, 256, 512, 512, 0
    else:
        Bn, stride, pos0 = attn_ctx["batch"], attn_ctx["batch"], attn_ctx["past_len"]
        tm_in = tm_ffn = tm_pool = T
    N = G * T
    tm_flat = min(N, 512)
    k_rows, v_rows, wkv_out, shift_out, pool_out, conv_out = [], [], [], [], [], []
    for l in range(DEPTH):
        gn = W["norm_mix"][l][None]
        if l % 2 == 0:
            e = l // 2
            lam_init = 0.8 - 0.6 * math.exp(-0.3 * l)
            prm = _layer_params(W, e)
            (k, v, qkv, s_new, nkk, d, b, kt, r, vb, g, bonus) = _inproj(
                x, gn, W["w_in_bf"][e], shift_in[e], prm, tm=tm_in, stride=stride)
            lq = W["diff_lambda"][e]
            sub = W["diff_subln"][e][None]
            if prompt:
                oa = _pattn(qkv, lq, sub, lam_init, tq=512).reshape(N, A_WIDTH)
                o, st = _wkv_chunked(nkk, d, b, kt, r, vb)
                o = o.reshape(N, B_WIDTH)
            else:
                nt = T // Bn
                bh = lambda a: a.reshape(nt, Bn, A_HEADS, 2 * A_HD).transpose(1, 2, 0, 3)
                padr = lambda a, rows: jnp.pad(a, ((0, 0), (0, 0), (0, rows - a.shape[2]), (0, 0)))
                q8 = padr(bh(qkv[0, :, :A_WIDTH]), SROWS // 2)
                comp = (jnp.arange(2 * A_HD) // A_HD)
                qh = jnp.concatenate([jnp.where(comp == c, q8, jnp.zeros((), BF16)) for c in range(2)], axis=2)
                oa = _sattn(attn_ctx["page_table"], qh, padr(bh(k[0]), NEW_KEY_ROWS), padr(bh(v[0]), NEW_KEY_ROWS),
                            lq, sub, attn_ctx["cache_k"], attn_ctx["cache_v"], e, nt, lam_init)
                oa = oa[:, :nt].transpose(1, 0, 2).reshape(N, A_WIDTH).astype(BF16)
                streams = [_to_wkv_sample(a[0], Bn) for a in (nkk, d, b, kt, r, vb)]
                o, st = _wkv(*streams, _state_to_wkv_sample(wkv_in[e]), tb=nt)
                o = _o_from_wkv_sample(o, Bn)
                st = _state_from_wkv_sample(st, Bn)
            x = _mixout(x.reshape(N, D_MODEL), oa, o, g.reshape(N, B_WIDTH), bonus.reshape(N, B_WIDTH),
                        W["rwkv_gn_w"][e][None], W["rwkv_gn_b"][e][None], prm["avg"], W["w_out_bf"][e],
                        tm=tm_flat).reshape(G, T, D_MODEL)
            k_rows.append(k)
            v_rows.append(v)
            wkv_out.append(st)
            shift_out.append(s_new)
        else:
            o_ = l // 2
            x, buf = _pool(x, gn, W["pool_w_bf"][o_], W["pool_scale"][o_][None], pool_in[o_],
                           tm=tm_pool, stride=stride, pos0=pos0)
            pool_out.append(buf)
        x, cbuf = _ffn(x, W["norm_ffn"][l][None], W["ffn_up_bf"][l], W["ffn_conv_w"][l],
                       W["ffn_conv_b"][l][None], W["ffn_down_bf"][l], conv_in[l], tm=tm_ffn, stride=stride)
        conv_out.append(cbuf)
    y = _final_norm(x.reshape(N, D_MODEL), W["norm_final"][None], tm=tm_flat).reshape(G, T, D_MODEL)
    return (y, jnp.stack(k_rows), jnp.stack(v_rows), jnp.stack(wkv_out), jnp.stack(shift_out),
            jnp.stack(pool_out), jnp.stack(conv_out))


def kernel(x_prompt, x_sample, cache_k, cache_v, state_wkv, state_shift, state_pool, state_ffn_conv,
           page_table, norm_mix, norm_ffn, norm_final, w_in, w_out, diff_lambda, diff_subln,
           rwkv_mu, rwkv_w0, rwkv_w2, rwkv_a0, rwkv_a2, rwkv_g2, rwkv_k_k, rwkv_k_a, rwkv_r_k,
           rwkv_gn_w, rwkv_gn_b, pool_w, pool_scale, ffn_up, ffn_conv_w, ffn_conv_b, ffn_down):
    W = dict(norm_mix=norm_mix, norm_ffn=norm_ffn, norm_final=norm_final,
             diff_lambda=diff_lambda, diff_subln=diff_subln, rwkv_mu=rwkv_mu, rwkv_w0=rwkv_w0,
             rwkv_w2=rwkv_w2, rwkv_a0=rwkv_a0, rwkv_a2=rwkv_a2, rwkv_g2=rwkv_g2, rwkv_k_k=rwkv_k_k,
             rwkv_k_a=rwkv_k_a, rwkv_r_k=rwkv_r_k, rwkv_gn_w=rwkv_gn_w, rwkv_gn_b=rwkv_gn_b,
             pool_scale=pool_scale, ffn_conv_w=ffn_conv_w, ffn_conv_b=ffn_conv_b,
             w_in_bf=w_in.astype(BF16), w_out_bf=w_out.astype(BF16), pool_w_bf=pool_w.astype(BF16),
             ffn_up_bf=ffn_up.astype(BF16), ffn_down_bf=ffn_down.astype(BF16))
    n_even, n_odd = state_wkv.shape[0], state_pool.shape[0]
    Bp, Sp, _ = x_prompt.shape
    Bs, Ts, _ = x_sample.shape
    n_pages = page_table.shape[1]
    past_len = n_pages * cache_k.shape[2]

    y_p, k_p, v_p, wkv_p, shift_p, pool_p, conv_p = _trunk(
        x_prompt, W, prompt=True, wkv_in=None,
        shift_in=jnp.zeros((n_even, Bp, 1, B_COLS), F32),
        pool_in=jnp.zeros((n_odd, Bp, POOL_BUF, D_MODEL), F32),
        conv_in=jnp.zeros((DEPTH, Bp, CONV_W - 1, D_FF), F32), attn_ctx=None)
    k_p = k_p.reshape(n_even, Bp, Sp, A_HEADS, 2 * A_HD)
    v_p = v_p.reshape(n_even, Bp, Sp, A_HEADS, 2 * A_HD)
    shift_p = shift_p.reshape(n_even, Bp, B_COLS)

    tmaj = lambda a: jnp.swapaxes(a, -3, -2)
    flat = lambda a: a.reshape(a.shape[:-3] + (1, a.shape[-3] * a.shape[-2], a.shape[-1]))
    ctx = dict(batch=Bs, past_len=past_len, page_table=page_table,
               cache_k=cache_k.reshape(cache_k.shape[:2] + (PAGE_SIZE * A_HEADS, 2 * A_HD)),
               cache_v=cache_v.reshape(cache_v.shape[:2] + (PAGE_SIZE * A_HEADS, 2 * A_HD)))
    y_s, k_s, v_s, wkv_s, shift_s, pool_s, conv_s = _trunk(
        flat(tmaj(x_sample)), W, prompt=False, wkv_in=state_wkv,
        shift_in=state_shift[:, None], pool_in=flat(tmaj(state_pool)),
        conv_in=flat(tmaj(state_ffn_conv)), attn_ctx=ctx)
    unflat = lambda a, j: tmaj(a.reshape(a.shape[:-3] + (j, Bs, a.shape[-1])))
    y_s = unflat(y_s, Ts)
    k_s = k_s.reshape(n_even, Ts, Bs, A_HEADS, 2 * A_HD).transpose(0, 2, 1, 3, 4)
    v_s = v_s.reshape(n_even, Ts, Bs, A_HEADS, 2 * A_HD).transpose(0, 2, 1, 3, 4)
    shift_s = shift_s.reshape(n_even, Bs, B_COLS)
    pool_s = unflat(pool_s, POOL_BUF)
    conv_s = unflat(conv_s, CONV_W - 1)
    return (y_p, y_s, k_p, v_p, wkv_p, shift_p, pool_p, conv_p,
            k_s, v_s, wkv_s, shift_s, pool_s, conv_s)
```

```python
import functools
import math

import jax
import jax.numpy as jnp
from jax import lax
from jax.experimental import pallas as pl
from jax.experimental.pallas import tpu as pltpu

F32 = jnp.float32
BF16 = jnp.bfloat16

D_MODEL = 1024
DEPTH = 4
PAGE_SIZE = 128
A_HD = 64
A_HEADS = 4
A_WIDTH = 512
A_COLS = 3 * A_WIDTH
A_SCALE = A_HD ** -0.5
Q_SCALE = A_SCALE * math.log2(math.e)
B_HD = 64
B_WIDTH = 512
B_HEADS = 8
DECAY_RANK = 64
ICLR_RANK = 64
GATE_RANK = 128
B_COLS = 3 * B_WIDTH + DECAY_RANK + ICLR_RANK + GATE_RANK
IN_COLS = A_COLS + B_COLS
POOL_WINDOWS = (2, 4, 8, 16)
POOL_GW = D_MODEL // len(POOL_WINDOWS)
POOL_BUF = max(POOL_WINDOWS) - 1
D_FF = 2816
CONV_W = 3
NORM_EPS = 1e-6
SUBLN_EPS = 1e-5
GN_EPS = 64e-5

LANES = 128
SUBLANES = 8
VMEM_LIMIT = 56 * 1024 * 1024
FF_CHUNK = 1408
PAGES_PER_STEP = 16
NEW_KEY_ROWS = 16
SROWS = 16
HEAD_LOG2 = 6


def _div_nonneg(x, n):
    if n == 1:
        return x
    if n & (n - 1) == 0:
        return lax.shift_right_logical(x, n.bit_length() - 1)
    return x // n


def _cparams(sem):
    return pltpu.CompilerParams(dimension_semantics=sem, vmem_limit_bytes=VMEM_LIMIT)


def _const_spec(shape):
    nd = len(shape)
    return pl.BlockSpec(shape, lambda *_: (0,) * nd, pipeline_mode=pl.Buffered(1))


def _rms(x, g, eps):
    return x * lax.rsqrt(jnp.mean(x * x, axis=-1, keepdims=True) + eps) * g


def _split_dot(x, m):
    hi = x.astype(BF16)
    lo = (x - hi.astype(F32)).astype(BF16)
    return (jnp.dot(hi, m, preferred_element_type=F32)
            + jnp.dot(lo, m, preferred_element_type=F32))


def _bf16_dot(x, m):
    return jnp.dot(x.astype(BF16), m, preferred_element_type=F32)


def _lane_tile(x, n):
    return jnp.concatenate([x] * n, axis=1)


def _softplus(z):
    return jnp.maximum(z, 0.0) + jnp.log1p(jnp.exp(-jnp.abs(z)))


def _inproj_kernel(x_ref, gn_ref, win_ref, sprev_ref, mu_ref, w0_ref, w2p_ref, a0_ref, a2p_ref,
                   g2_ref, kk_ref, ka_ref, rk_ref, seg_ref,
                   k_ref, v_ref, qkv_ref, sout_ref, nkk_ref, d_ref, b_ref, kt_ref, r_ref,
                   vb_ref, g_ref, bonus_ref, ext_ref, *, tm, stride, halo):
    s = pl.program_id(1)
    ns = pl.num_programs(1)
    h = _rms(x_ref[...], gn_ref[...], NORM_EPS).astype(BF16)
    p = jnp.dot(h, win_ref[...], preferred_element_type=F32)
    for hh in range(A_HEADS):
        c0 = A_WIDTH + hh * 2 * A_HD
        k_ref[pl.ds(hh, tm, stride=A_HEADS), :] = p[:, c0:c0 + 2 * A_HD]
        v_ref[pl.ds(hh, tm, stride=A_HEADS), :] = p[:, A_WIDTH + c0:A_WIDTH + c0 + 2 * A_HD]
    qkv_ref[:, :A_WIDTH] = (p[:, :A_WIDTH] * Q_SCALE).astype(BF16)
    qkv_ref[:, A_WIDTH:] = p[:, A_WIDTH:A_COLS].astype(BF16)
    pb = p[:, A_COLS:]

    @pl.when(s == 0)
    def _():
        ext_ref[halo - stride:halo, :] = sprev_ref[...]

    ext_ref[halo:halo + tm, :] = pb
    prev = ext_ref[halo - stride:halo - stride + tm, :]
    ext_ref[halo - stride:halo, :] = ext_ref[halo + tm - stride:halo + tm, :]

    @pl.when(s == ns - 1)
    def _():
        sout_ref[...] = ext_ref[halo + tm - stride:halo + tm, :]

    xm = pb + (prev - pb) * mu_ref[...]
    r = xm[:, :B_WIDTH]
    k = xm[:, B_WIDTH:2 * B_WIDTH]
    v = xm[:, 2 * B_WIDTH:3 * B_WIDTH]
    wa = xm[:, 3 * B_WIDTH:3 * B_WIDTH + DECAY_RANK + ICLR_RANK]
    gd = xm[:, 3 * B_WIDTH + DECAY_RANK + ICLR_RANK:]
    wlin = w0_ref[...] + jnp.dot(jnp.tanh(wa).astype(BF16), w2p_ref[...], preferred_element_type=F32)
    w = -_softplus(-wlin) - 0.5
    log_decay = -jnp.exp(w)
    a = jax.nn.sigmoid(a0_ref[...] + jnp.dot(wa.astype(BF16), a2p_ref[...], preferred_element_type=F32))
    g = jnp.dot(jax.nn.sigmoid(gd).astype(BF16), g2_ref[...], preferred_element_type=F32)
    kk = k * kk_ref[...]
    ss = _bf16_dot(kk * kk, seg_ref[...])
    kk = kk / jnp.maximum(jnp.sqrt(ss), 1e-12)
    kt = k * (1.0 + (a - 1.0) * ka_ref[...])
    bonus = _bf16_dot(r * kt * rk_ref[...], seg_ref[...]) * v
    nkk_ref[...] = -kk
    d_ref[...] = log_decay
    b_ref[...] = kk * a
    kt_ref[...] = kt
    r_ref[...] = r
    vb_ref[...] = v
    g_ref[...] = g
    bonus_ref[...] = bonus


def _inproj(x, gn, win, sprev, prm, *, tm, stride):
    G, T, _ = x.shape
    halo = max(SUBLANES, stride)
    ns = T // tm
    row = lambda w: pl.BlockSpec((None, tm, w), lambda g, s: (g, s, 0))
    stream = jax.ShapeDtypeStruct((G, T, B_WIDTH), F32)
    in_specs = [row(D_MODEL), _const_spec((1, D_MODEL)), _const_spec((D_MODEL, IN_COLS)),
                pl.BlockSpec((None, stride, B_COLS), lambda g, s: (g, 0, 0)),
                _const_spec((1, B_COLS)), _const_spec((1, B_WIDTH)), _const_spec((LANES, B_WIDTH)),
                _const_spec((1, B_WIDTH)), _const_spec((LANES, B_WIDTH)), _const_spec((GATE_RANK, B_WIDTH)),
                _const_spec((1, B_WIDTH)), _const_spec((1, B_WIDTH)), _const_spec((1, B_WIDTH)),
                _const_spec((B_WIDTH, B_WIDTH))]
    kv_shape = jax.ShapeDtypeStruct((G, T * A_HEADS, 2 * A_HD), F32)
    kv_spec = pl.BlockSpec((None, tm * A_HEADS, 2 * A_HD), lambda g, s: (g, s, 0))
    out_shape = [kv_shape, kv_shape, jax.ShapeDtypeStruct((G, T, A_COLS), BF16),
                 jax.ShapeDtypeStruct((G, stride, B_COLS), F32)] + [stream] * 8
    out_specs = [kv_spec, kv_spec, row(A_COLS),
                 pl.BlockSpec((None, stride, B_COLS), lambda g, s: (g, 0, 0))] + [row(B_WIDTH)] * 8
    return pl.pallas_call(
        functools.partial(_inproj_kernel, tm=tm, stride=stride, halo=halo),
        grid=(G, ns), in_specs=in_specs, out_specs=out_specs, out_shape=out_shape,
        scratch_shapes=[pltpu.VMEM((halo + tm, B_COLS), F32)],
        compiler_params=_cparams(("arbitrary", "arbitrary")), name="inproj",
    )(x, gn, win, sprev, prm["mu"], prm["w0"], prm["w2p"], prm["a0"], prm["a2p"], prm["g2"],
      prm["k_k"], prm["k_a"], prm["r_k"], prm["seg"])


def _lambda_full(lq, lam_init):
    s01 = jnp.sum(lq[0:1, :] * lq[1:2, :], axis=-1, keepdims=True)
    s23 = jnp.sum(lq[2:3, :] * lq[3:4, :], axis=-1, keepdims=True)
    return jnp.exp(s01) - jnp.exp(s23) + lam_init


def _pattn_kernel(qi_ref, kj_ref, q_ref, k_ref, v_ref, lq_ref, sub_ref, o_ref,
                  m0_ref, l0_ref, a0_ref, m1_ref, l1_ref, a1_ref, *, tq, lam_init):
    qi = qi_ref[pl.program_id(2)]
    kj = kj_ref[pl.program_id(2)]

    @pl.when(kj == 0)
    def _():
        for m_ref, l_ref, a_ref in ((m0_ref, l0_ref, a0_ref), (m1_ref, l1_ref, a1_ref)):
            m_ref[...] = jnp.full(m_ref.shape, -jnp.inf, F32)
            l_ref[...] = jnp.zeros(l_ref.shape, F32)
            a_ref[...] = jnp.zeros(a_ref.shape, F32)

    def accumulate(diagonal):
        q = q_ref[...]
        k = k_ref[...]
        v = v_ref[...]
        lane = lax.broadcasted_iota(jnp.int32, q.shape, 1)
        zero = jnp.zeros_like(q)
        if diagonal:
            keep = (lax.broadcasted_iota(jnp.int32, (tq, tq), 1)
                    <= lax.broadcasted_iota(jnp.int32, (tq, tq), 0))
        refs = ((m0_ref, l0_ref, a0_ref), (m1_ref, l1_ref, a1_ref))
        comps = range(2)
        qc = [jnp.where((lane >= c * A_HD) & (lane < (c + 1) * A_HD), q, zero) for c in comps]
        s = [lax.dot_general(qc[c], k, (((1,), (1,)), ((), ())), preferred_element_type=F32) for c in comps]
        if diagonal:
            s = [jnp.where(keep, s[c], -jnp.inf) for c in comps]
        m_old = [refs[c][0][...] for c in comps]
        m_new = [jnp.maximum(m_old[c], jnp.max(s[c], axis=-1, keepdims=True)) for c in comps]
        alpha = [jnp.exp2(m_old[c] - m_new[c]) for c in comps]
        p = [jnp.exp2(s[c] - _lane_tile(m_new[c], tq // LANES)) for c in comps]
        for c in comps:
            m_ref, l_ref, a_ref = refs[c]
            l_ref[...] = alpha[c] * l_ref[...] + jnp.sum(p[c], axis=-1, keepdims=True)
            a_ref[...] = alpha[c] * a_ref[...] + jnp.dot(p[c].astype(BF16), v, preferred_element_type=F32)
            m_ref[...] = m_new[c]

    @pl.when(kj < qi)
    def _():
        accumulate(False)

    @pl.when(kj == qi)
    def _():
        accumulate(True)
        lam = _lambda_full(lq_ref[...], lam_init)
        o = a0_ref[...] / l0_ref[...] - lam * (a1_ref[...] / l1_ref[...])
        o = _rms(o, sub_ref[...], SUBLN_EPS) * (1.0 - lam_init)
        o_ref[...] = o.astype(o_ref.dtype)


def _pattn(qkv, lq, sub, lam_init, *, tq):
    B, S, _ = qkv.shape
    nq = S // tq
    nh = A_HEADS
    pairs = [(i, j) for i in range(nq) for j in range(i + 1)]
    qi_tab = jnp.asarray([p[0] for p in pairs], jnp.int32)
    kj_tab = jnp.asarray([p[1] for p in pairs], jnp.int32)
    q_spec = pl.BlockSpec((None, tq, LANES), lambda b, h, n, qt, kt: (b, qt[n], h))
    k_spec = pl.BlockSpec((None, tq, LANES), lambda b, h, n, qt, kt: (b, kt[n], nh + h))
    v_spec = pl.BlockSpec((None, tq, LANES), lambda b, h, n, qt, kt: (b, kt[n], 2 * nh + h))
    grid_spec = pltpu.PrefetchScalarGridSpec(
        num_scalar_prefetch=2, grid=(B, nh, len(pairs)),
        in_specs=[q_spec, k_spec, v_spec,
                  pl.BlockSpec((4, A_HD), lambda b, h, n, qt, kt: (0, 0)),
                  pl.BlockSpec((1, LANES), lambda b, h, n, qt, kt: (0, 0))],
        out_specs=pl.BlockSpec((None, tq, LANES), lambda b, h, n, qt, kt: (b, qt[n], h)),
        scratch_shapes=[pltpu.VMEM((tq, LANES), F32)] * 6)
    return pl.pallas_call(
        functools.partial(_pattn_kernel, tq=tq, lam_init=lam_init),
        grid_spec=grid_spec, out_shape=jax.ShapeDtypeStruct((B, S, A_WIDTH), BF16),
        compiler_params=_cparams(("arbitrary",) * 3), name="pattn",
    )(qi_tab, kj_tab, qkv, qkv, qkv, lq, sub)


def _sattn_kernel(pt_ref, q_ref, kn_ref, vn_ref, lq_ref, sub_ref, *rest, npg, nt, lam_init):
    k_refs = rest[:npg]
    v_refs = rest[npg:2 * npg]
    o_ref = rest[2 * npg]
    m_ref, l_ref, acc_ref = rest[2 * npg + 1:]
    j = pl.program_id(1)
    nj = pl.num_programs(1)

    @pl.when(j == 0)
    def _():
        m_ref[...] = jnp.full(m_ref.shape, -jnp.inf, F32)
        l_ref[...] = jnp.zeros(l_ref.shape, F32)
        acc_ref[...] = jnp.zeros(acc_ref.shape, F32)

    nt_dims = (((1,), (1,)), ((), ()))
    qs = [q_ref[h] for h in range(A_HEADS)]

    def head_rows(pg_ref, h):
        return pg_ref[pl.ds(h, PAGE_SIZE, stride=A_HEADS), :].astype(BF16)

    def update(s, pv):
        m_old = m_ref[...]
        m_new = jnp.maximum(m_old, jnp.max(s, axis=-1, keepdims=True))
        alpha = jnp.exp2(m_old - m_new)
        p = jnp.exp2(s - (_lane_tile(m_new, s.shape[1] // LANES) if s.shape[1] > LANES
                          else m_new[:, :s.shape[1]]))
        l_ref[...] = alpha * l_ref[...] + jnp.sum(p, axis=-1, keepdims=True)
        acc_ref[...] = alpha * acc_ref[...] + pv(p.astype(BF16))
        m_ref[...] = m_new

    s_heads = [jnp.concatenate(
        [lax.dot_general(qs[h], head_rows(kr, h), nt_dims, preferred_element_type=F32) for kr in k_refs],
        axis=1) for h in range(A_HEADS)]

    def pv_pages(p):
        outs = []
        for h in range(A_HEADS):
            ph = p[h * SROWS:(h + 1) * SROWS, :]
            tot = None
            for i, vr in enumerate(v_refs):
                part = jnp.dot(ph[:, i * PAGE_SIZE:(i + 1) * PAGE_SIZE], head_rows(vr, h),
                               preferred_element_type=F32)
                tot = part if tot is None else tot + part
            outs.append(tot)
        return jnp.concatenate(outs, axis=0)

    update(jnp.concatenate(s_heads, axis=0), pv_pages)

    @pl.when(j == nj - 1)
    def _():
        s_new = jnp.concatenate(
            [lax.dot_general(qs[h], kn_ref[h].astype(BF16), nt_dims, preferred_element_type=F32)
             for h in range(A_HEADS)], axis=0)
        key = lax.broadcasted_iota(jnp.int32, s_new.shape, 1)
        tok = lax.broadcasted_iota(jnp.int32, s_new.shape, 0) & (SROWS // 2 - 1)
        s_new = jnp.where((key <= tok) & (key < nt), s_new, -jnp.inf)

        def pv_new(p):
            return jnp.concatenate(
                [jnp.dot(p[h * SROWS:(h + 1) * SROWS, :], vn_ref[h].astype(BF16), preferred_element_type=F32)
                 for h in range(A_HEADS)], axis=0)

        update(s_new, pv_new)
        lam = _lambda_full(lq_ref[...], lam_init)
        half = SROWS // 2
        outs = []
        for h in range(A_HEADS):
            r0 = h * SROWS
            o = (acc_ref[r0:r0 + half, :] / l_ref[r0:r0 + half, :]
                 - lam * (acc_ref[r0 + half:r0 + SROWS, :] / l_ref[r0 + half:r0 + SROWS, :]))
            outs.append(_rms(o, sub_ref[...], SUBLN_EPS) * (1.0 - lam_init))
        o_ref[...] = jnp.concatenate(outs, axis=-1)


def _sattn(page_table, qh, k_new, v_new, lq, sub, kc, vc, e, nt, lam_init):
    Bn = qh.shape[0]
    n_pages = page_table.shape[1]
    npg = PAGES_PER_STEP
    nj = n_pages // npg
    page_rows = PAGE_SIZE * A_HEADS
    per_b = lambda r: pl.BlockSpec((None, A_HEADS, r, LANES), lambda b, j, pt: (b, 0, 0, 0))

    def page_spec(i):
        return pl.BlockSpec((None, None, page_rows, LANES), lambda b, j, pt: (e, pt[b, j * npg + i], 0, 0))

    in_specs = ([per_b(SROWS), per_b(NEW_KEY_ROWS), per_b(NEW_KEY_ROWS),
                 pl.BlockSpec((4, A_HD), lambda b, j, pt: (0, 0)),
                 pl.BlockSpec((1, LANES), lambda b, j, pt: (0, 0))]
                + [page_spec(i) for i in range(npg)] * 2)
    nr = A_HEADS * SROWS
    grid_spec = pltpu.PrefetchScalarGridSpec(
        num_scalar_prefetch=1, grid=(Bn, nj), in_specs=in_specs,
        out_specs=pl.BlockSpec((None, SROWS // 2, A_WIDTH), lambda b, j, pt: (b, 0, 0)),
        scratch_shapes=[pltpu.VMEM((nr, LANES), F32)] * 3)
    return pl.pallas_call(
        functools.partial(_sattn_kernel, npg=npg, nt=nt, lam_init=lam_init),
        grid_spec=grid_spec, out_shape=jax.ShapeDtypeStruct((Bn, SROWS // 2, A_WIDTH), F32),
        compiler_params=_cparams(("arbitrary", "arbitrary")), name="sattn",
    )(page_table, qh, k_new, v_new, lq, sub, *([kc] * npg), *([vc] * npg))


def _wkv_kernel(nkk_ref, d_ref, b_ref, kt_ref, r_ref, v_ref, s0_ref, o_ref, st_ref, s_scr, *, tb, nv):
    t0 = pl.program_id(1)

    @pl.when(t0 == 0)
    def _():
        s_scr[...] = s0_ref[...]

    def step(t, carry):
        nkk = nkk_ref[t]
        d = jnp.exp(d_ref[t])
        b = b_ref[t]
        kt = kt_ref[t]
        r = r_ref[t]
        dr = d * r
        br = jnp.sum(b * r, axis=0, keepdims=True)
        kr = jnp.sum(kt * r, axis=0, keepdims=True)

        def row(vr, c):
            S = s_scr[vr]
            sa = jnp.sum(S * nkk, axis=0, keepdims=True)
            qo = jnp.sum(S * dr, axis=0, keepdims=True)
            vv = v_ref[t, pl.ds(vr, 1), :]
            s_scr[vr] = S * d + sa * b + vv * kt
            o_ref[t, pl.ds(vr, 1), :] = qo + sa * br + vv * kr
            return c

        return lax.fori_loop(0, nv, row, carry, unroll=2)

    lax.fori_loop(0, tb, step, 0)
    st_ref[...] = s_scr[...]


def _wkv(nkk, d, b, kt, r, v, s0, *, tb):
    LG, T, _, _ = nkk.shape
    nv = v.shape[2]
    kspec = pl.BlockSpec((None, tb, B_HD, LANES), lambda g, t: (g, t, 0, 0))
    vspec = pl.BlockSpec((None, tb, nv, LANES), lambda g, t: (g, t, 0, 0))
    sspec = pl.BlockSpec((None, nv, B_HD, LANES), lambda g, t: (g, 0, 0, 0))
    return pl.pallas_call(
        functools.partial(_wkv_kernel, tb=tb, nv=nv),
        grid=(LG, T // tb), in_specs=[kspec] * 5 + [vspec, sspec],
        out_specs=[vspec, sspec],
        out_shape=[jax.ShapeDtypeStruct((LG, T, nv, LANES), F32),
                   jax.ShapeDtypeStruct((LG, nv, B_HD, LANES), F32)],
        scratch_shapes=[pltpu.VMEM((nv, B_HD, LANES), F32)],
        compiler_params=_cparams(("arbitrary", "arbitrary")), name="wkv",
    )(nkk, d, b, kt, r, v, s0)


WKV_CHUNK = 64
WKV_GROUP = 4


def _split3_dot(m, x):
    x1 = x.astype(BF16)
    r1 = x - x1.astype(F32)
    x2 = r1.astype(BF16)
    x3 = (r1 - x2.astype(F32)).astype(BF16)
    return (jnp.dot(m, x1, preferred_element_type=F32) + jnp.dot(m, x2, preferred_element_type=F32)
            + jnp.dot(m, x3, preferred_element_type=F32))


def _wkv_chunk_math(a, ld, b, kt, r, v, s):
    L, HW = WKV_CHUNK, WKV_GROUP * B_HD
    nt_dims = (((1,), (1,)), ((), ()))
    tn_dims = (((0,), (0,)), ((), ()))
    row = lax.broadcasted_iota(jnp.int32, (L, HW), 0)
    col = lax.broadcasted_iota(jnp.int32, (L, HW), 1) & (B_HD - 1)
    strict = col < row
    incl = col <= row
    blk = lax.shift_right_logical
    bd_mask = (lax.shift_right_logical(lax.broadcasted_iota(jnp.int32, (HW, HW), 0), HEAD_LOG2)
               == lax.shift_right_logical(lax.broadcasted_iota(jnp.int32, (HW, HW), 1), HEAD_LOG2))

    def bd(y):
        y = y.astype(BF16)
        return jnp.where(bd_mask, jnp.concatenate([y] * WKV_GROUP, axis=0), jnp.zeros((), BF16))

    def hprod(x, y):
        return jnp.dot(x.astype(BF16), bd(y), preferred_element_type=F32)

    def each(f, *lists):
        return [f(*xs) for xs in zip(*lists)]

    def dotg(dims):
        return lambda x, y: lax.dot_general(x, y, dims, preferred_element_type=F32)

    tri = (lax.broadcasted_iota(jnp.int32, (L, L), 1) <= lax.broadcasted_iota(jnp.int32, (L, L), 0)).astype(BF16)
    logp = each(lambda x: _split3_dot(tri, x), ld)
    ep = each(jnp.exp, logp)
    p_last = each(lambda x: x[L - 1:L, :], ep)
    en = each(lambda x: jnp.exp(-x), logp)
    at = each(lambda x, lp, l_: x * jnp.exp(lp - l_), a, logp, ld)
    rt = each(jnp.multiply, r, ep)
    bt = each(jnp.multiply, b, en)
    ktt = each(jnp.multiply, kt, en)
    bh = each(lambda x, pp: (x * pp).astype(BF16), bt, p_last)
    kh = each(lambda x, pp: (x * pp).astype(BF16), ktt, p_last)

    lhs = each(lambda x, y: jnp.concatenate([x, y], axis=0).astype(BF16), at, rt)
    xb = each(lambda x, y: dotg(nt_dims)(x, bd(y)), lhs, bt)
    xk = each(lambda x, y: dotg(nt_dims)(x, bd(y)), lhs, ktt)
    n = each(lambda x: jnp.where(strict, x[:L], 0.0), xb)
    arb = each(lambda x: jnp.where(incl, x[L:], 0.0), xb)
    aak = each(lambda x: jnp.where(strict, x[:L], 0.0), xk)
    ark = each(lambda x: jnp.where(incl, x[L:], 0.0), xk)

    eye = jnp.where(col == row, 1.0, 0.0)
    tm = each(lambda x: eye + jnp.where(blk(row, 1) == blk(col, 1), x, 0.0), n)
    for lg in range(1, HEAD_LOG2):
        sel = (blk(row, lg + 1) == blk(col, lg + 1)) & (blk(row, lg) != blk(col, lg))
        w = each(lambda t_, x: hprod(t_, jnp.where(sel, x, 0.0)), tm, n)
        tm = each(lambda t_, w_: t_ + hprod(w_, t_), tm, w)

    rloc = each(hprod, aak, v)
    uloc = each(hprod, tm, rloc)
    ah = each(hprod, tm, at)
    ro = each(lambda x, y, z: (x + hprod(y, z)).astype(BF16), rt, arb, ah)
    oc = each(lambda x, u, y, vv: hprod(x, u) + hprod(y, vv), arb, uloc, ark, v)
    mpp = each(lambda x, y: jnp.where(bd_mask, dotg(tn_dims)(x.astype(BF16), y), 0.0).astype(BF16), ah, bh)
    sc = each(lambda u, y, vv, z: jnp.where(
        bd_mask, dotg(tn_dims)(u.astype(BF16), y) + dotg(tn_dims)(vv.astype(BF16), z), 0.0), uloc, bh, v, kh)

    s_hi = each(lambda x: x.astype(BF16), s)
    s_lo = each(lambda x, h: (x - h.astype(F32)).astype(BF16), s, s_hi)
    o = each(lambda q, h, l_, c_: dotg(nt_dims)(q, h) + dotg(nt_dims)(q, l_) + c_, ro, s_hi, s_lo, oc)
    s_new = each(lambda x, pp, h, l_, m_, c_: (x * pp + jnp.dot(h, m_, preferred_element_type=F32)
                                               + jnp.dot(l_, m_, preferred_element_type=F32) + c_),
                 s, p_last, s_hi, s_lo, mpp, sc)
    return o, s_new


def _wkv_chunk_kernel(a_ref, ld_ref, b_ref, kt_ref, r_ref, v_ref, o_ref, st_ref, s_scr, *, nb):
    c = pl.program_id(1)
    nc = pl.num_programs(1)
    HW = WKV_GROUP * B_HD

    @pl.when(c == 0)
    def _():
        s_scr[...] = jnp.zeros(s_scr.shape, F32)

    groups = [(bb, g) for bb in range(nb) for g in range(B_WIDTH // HW)]
    sls = [(bb, slice(None), slice(g * HW, (g + 1) * HW)) for bb, g in groups]
    o, s_new = _wkv_chunk_math(*[[ref[sl] for sl in sls] for ref in (a_ref, ld_ref, b_ref, kt_ref, r_ref, v_ref)],
                               [s_scr[bb, g] for bb, g in groups])
    for sl, (bb, g), o_i, s_i in zip(sls, groups, o, s_new):
        o_ref[sl] = o_i
        s_scr[bb, g] = s_i

    @pl.when(c == nc - 1)
    def _():
        st_ref[...] = s_scr[...]


def _wkv_chunked(a, ld, b, kt, r, v, *, nb=4):
    Bn, T, _ = a.shape
    L, HW = WKV_CHUNK, WKV_GROUP * B_HD
    ng = B_WIDTH // HW
    nb = min(nb, Bn)
    assert Bn % nb == 0 and T % L == 0
    spec = pl.BlockSpec((nb, L, B_WIDTH), lambda bb, c: (bb, c, 0))
    st_spec = pl.BlockSpec((nb, ng, HW, HW), lambda bb, c: (bb, 0, 0, 0))
    o, st = pl.pallas_call(
        functools.partial(_wkv_chunk_kernel, nb=nb),
        grid=(Bn // nb, T // L), in_specs=[spec] * 6, out_specs=[spec, st_spec],
        out_shape=[jax.ShapeDtypeStruct((Bn, T, B_WIDTH), F32), jax.ShapeDtypeStruct((Bn, ng, HW, HW), F32)],
        scratch_shapes=[pltpu.VMEM((nb, ng, HW, HW), F32)],
        compiler_params=_cparams(("arbitrary",) * 2), name="wkv_chunked",
    )(a, ld, b, kt, r, v)
    st = st.reshape(Bn, ng, WKV_GROUP, B_HD, WKV_GROUP, B_HD)
    st = jnp.stack([st[:, :, i, :, i, :] for i in range(WKV_GROUP)], axis=2)
    return o, st.reshape(Bn, B_HEADS, B_HD, B_HD)


def _mixout_kernel(x_ref, oa_ref, o_ref, g_ref, bonus_ref, gnw_ref, gnb_ref, avg_ref, wout_ref, xo_ref):
    o = o_ref[...]
    mu = _split_dot(o, avg_ref[...])
    dlt = o - mu
    var = _bf16_dot(dlt * dlt, avg_ref[...])
    on = dlt * lax.rsqrt(var + GN_EPS) * gnw_ref[...] + gnb_ref[...]
    ob = ((on + bonus_ref[...]) * g_ref[...]).astype(BF16)
    y = (jnp.dot(oa_ref[...], wout_ref[0:A_WIDTH, :], preferred_element_type=F32)
         + jnp.dot(ob, wout_ref[A_WIDTH:, :], preferred_element_type=F32))
    xo_ref[...] = x_ref[...] + y


def _mixout(x, oa, o, g, bonus, gnw, gnb, avg, wout, *, tm):
    N = x.shape[0]
    row = lambda w: pl.BlockSpec((tm, w), lambda i: (i, 0))
    return pl.pallas_call(
        _mixout_kernel, grid=(N // tm,),
        in_specs=[row(D_MODEL), row(A_WIDTH), row(B_WIDTH), row(B_WIDTH), row(B_WIDTH),
                  _const_spec((1, B_WIDTH)), _const_spec((1, B_WIDTH)),
                  _const_spec((B_WIDTH, B_WIDTH)), _const_spec((D_MODEL, D_MODEL))],
        out_specs=row(D_MODEL), out_shape=jax.ShapeDtypeStruct((N, D_MODEL), F32),
        compiler_params=_cparams(("arbitrary",)), name="mixout",
    )(x, oa, o, g, bonus, gnw, gnb, avg, wout)


def _pool_kernel(x_ref, gn_ref, pw_ref, ps_ref, pprev_ref, xo_ref, pout_ref, ext_ref,
                 *, tm, stride, halo, pos0):
    s = pl.program_id(1)
    ns = pl.num_programs(1)
    keep = POOL_BUF * stride
    x = x_ref[...]
    h = _rms(x, gn_ref[...], NORM_EPS)

    @pl.when(s == 0)
    def _():
        ext_ref[halo - keep:halo, :] = pprev_ref[...]

    ext_ref[halo:halo + tm, :] = h
    t = _div_nonneg(s * tm + lax.broadcasted_iota(jnp.int32, (tm, 1), 0), stride)
    ys = []
    for gi, win in enumerate(POOL_WINDOWS):
        c0, c1 = gi * POOL_GW, (gi + 1) * POOL_GW
        cur = ext_ref[halo:halo + tm, c0:c1]
        wsum = cur
        for jj in range(1, win):
            wsum = wsum + ext_ref[halo - jj * stride:halo - jj * stride + tm, c0:c1]
        cnt = jnp.minimum(pos0 + t + 1, win).astype(F32)
        m = wsum / cnt - cur
        ys.append(jnp.dot(m.astype(BF16), pw_ref[gi], preferred_element_type=F32))
    xo_ref[...] = x + jnp.concatenate(ys, axis=-1) * ps_ref[...]

    @pl.when(s == ns - 1)
    def _():
        pout_ref[...] = ext_ref[halo + tm - keep:halo + tm, :]

    if tm >= keep:
        ext_ref[halo - keep:halo, :] = ext_ref[halo + tm - keep:halo + tm, :]


def _pool(x, gn, pw, ps, pprev, *, tm, stride, pos0):
    G, T, _ = x.shape
    halo = 2 * SUBLANES * stride
    keep = POOL_BUF * stride
    assert T // tm == 1 or tm >= keep
    row = pl.BlockSpec((None, tm, D_MODEL), lambda g, s: (g, s, 0))
    st = pl.BlockSpec((None, keep, D_MODEL), lambda g, s: (g, 0, 0))
    return pl.pallas_call(
        functools.partial(_pool_kernel, tm=tm, stride=stride, halo=halo, pos0=pos0),
        grid=(G, T // tm),
        in_specs=[row, _const_spec((1, D_MODEL)), _const_spec((len(POOL_WINDOWS), POOL_GW, POOL_GW)),
                  _const_spec((1, D_MODEL)), st],
        out_specs=[row, st],
        out_shape=[jax.ShapeDtypeStruct((G, T, D_MODEL), F32), jax.ShapeDtypeStruct((G, keep, D_MODEL), F32)],
        scratch_shapes=[pltpu.VMEM((halo + tm, D_MODEL), F32)],
        compiler_params=_cparams(("arbitrary", "arbitrary")), name="pool",
    )(x, gn, pw, ps, pprev)


def _ffn_kernel(x_ref, gn_ref, wup_ref, cw_ref, cb_ref, wdn_ref, cprev_ref, *rest, tm, stride, halo, out_norm):
    gout_ref = rest[0] if out_norm else None
    xo_ref, cout_ref, ext_ref = rest[-3:]
    s = pl.program_id(1)
    ns = pl.num_programs(1)
    keep = (CONV_W - 1) * stride
    x = x_ref[...]
    h = _rms(x, gn_ref[...], NORM_EPS).astype(BF16)

    @pl.when(s == 0)
    def _():
        ext_ref[halo - keep:halo, :] = cprev_ref[...]

    def up(c):
        c0, c1 = c * FF_CHUNK, (c + 1) * FF_CHUNK
        gate = jnp.dot(h, wup_ref[:, c0:c1], preferred_element_type=F32)
        val = jnp.dot(h, wup_ref[:, D_FF + c0:D_FF + c1], preferred_element_type=F32)
        ext_ref[halo:halo + tm, c0:c1] = gate
        return gate, val

    def down(c, gate, val):
        c0, c1 = c * FF_CHUNK, (c + 1) * FF_CHUNK
        cc = cb_ref[:, c0:c1]
        for jj in range(CONV_W - 1):
            back = (CONV_W - 1 - jj) * stride
            cc = cc + ext_ref[halo - back:halo - back + tm, c0:c1] * cw_ref[jj:jj + 1, c0:c1]
        cc = cc + gate * cw_ref[CONV_W - 1:CONV_W, c0:c1]
        act = (0.5 * cc * (1.0 + lax.erf(cc * (2.0 ** -0.5))) * val).astype(BF16)
        return jnp.dot(act, wdn_ref[c0:c1, :], preferred_element_type=F32)

    nchunk = D_FF // FF_CHUNK
    acc = x
    pending = up(0)
    for c in range(nchunk):
        nxt = up(c + 1) if c + 1 < nchunk else None
        acc = acc + down(c, *pending)
        pending = nxt
    xo_ref[...] = _rms(acc, gout_ref[...], NORM_EPS) if out_norm else acc

    @pl.when(s == ns - 1)
    def _():
        cout_ref[...] = ext_ref[halo + tm - keep:halo + tm, :]

    ext_ref[halo - keep:halo, :] = ext_ref[halo + tm - keep:halo + tm, :]


def _ffn(x, gn, wup, cw, cb, wdn, cprev, *, tm, stride, out_gain=None):
    G, T, _ = x.shape
    keep = (CONV_W - 1) * stride
    halo = max(SUBLANES, keep)
    assert tm >= keep
    row = pl.BlockSpec((None, tm, D_MODEL), lambda g, s: (g, s, 0))
    st = pl.BlockSpec((None, keep, D_FF), lambda g, s: (g, 0, 0))
    out_norm = out_gain is not None
    extra_specs, extra_args = ([_const_spec((1, D_MODEL))], [out_gain]) if out_norm else ([], [])
    return pl.pallas_call(
        functools.partial(_ffn_kernel, tm=tm, stride=stride, halo=halo, out_norm=out_norm),
        grid=(G, T // tm),
        in_specs=[row, _const_spec((1, D_MODEL)), _const_spec((D_MODEL, 2 * D_FF)),
                  _const_spec((CONV_W, D_FF)), _const_spec((1, D_FF)), _const_spec((D_FF, D_MODEL)), st]
        + extra_specs,
        out_specs=[row, st],
        out_shape=[jax.ShapeDtypeStruct((G, T, D_MODEL), F32), jax.ShapeDtypeStruct((G, keep, D_FF), F32)],
        scratch_shapes=[pltpu.VMEM((halo + tm, D_FF), F32)],
        compiler_params=_cparams(("arbitrary", "arbitrary")), name="ffn",
    )(x, gn, wup, cw, cb, wdn, cprev, *extra_args)


def _to_wkv_sample(a, Bn):
    T = a.shape[0] // Bn
    a = a.reshape(T, Bn, B_HEADS, B_HD).transpose(0, 3, 1, 2).reshape(T, B_HD, Bn * B_HEADS // LANES, LANES)
    return a.transpose(2, 0, 1, 3)


def _o_from_wkv_sample(o, Bn):
    LG, T = o.shape[0], o.shape[1]
    o = o.transpose(1, 2, 0, 3).reshape(T, B_HD, Bn, B_HEADS).transpose(0, 2, 3, 1)
    return o.reshape(T * Bn, B_WIDTH)


def _state_to_wkv_sample(st):
    Bn = st.shape[0]
    st = st.transpose(2, 3, 0, 1).reshape(B_HD, B_HD, Bn * B_HEADS // LANES, LANES)
    return st.transpose(2, 0, 1, 3)


def _state_from_wkv_sample(st, Bn):
    st = st.transpose(1, 2, 0, 3).reshape(B_HD, B_HD, Bn, B_HEADS)
    return st.transpose(2, 3, 0, 1)


def _layer_params(W, e):
    z = jnp.zeros((DECAY_RANK, B_WIDTH), F32)
    head = jnp.arange(B_WIDTH) // B_HD
    same = (head[:, None] == head[None, :])
    return {
        "mu": W["rwkv_mu"][e][None], "w0": W["rwkv_w0"][e][None],
        "w2p": jnp.concatenate([W["rwkv_w2"][e], z], axis=0).astype(BF16),
        "a0": W["rwkv_a0"][e][None],
        "a2p": jnp.concatenate([z, W["rwkv_a2"][e]], axis=0).astype(BF16),
        "g2": W["rwkv_g2"][e].astype(BF16),
        "k_k": W["rwkv_k_k"][e][None], "k_a": W["rwkv_k_a"][e][None],
        "r_k": W["rwkv_r_k"][e].reshape(1, B_WIDTH),
        "seg": same.astype(BF16), "avg": (same.astype(F32) / B_HD).astype(BF16),
    }


def _trunk(x, W, *, prompt, wkv_in, shift_in, pool_in, conv_in, attn_ctx):
    G, T, _ = x.shape
    if prompt:
        Bn, stride, tm_in, tm_ffn, tm_pool, pos0 = G, 1, 256, 512, 512, 0
    else:
        Bn, stride, pos0 = attn_ctx["batch"], attn_ctx["batch"], attn_ctx["past_len"]
        tm_in = tm_ffn = tm_pool = T
    N = G * T
    tm_flat = min(N, 512)
    k_rows, v_rows, wkv_out, shift_out, pool_out, conv_out = [], [], [], [], [], []
    for l in range(DEPTH):
        gn = W["norm_mix"][l][None]
        if l % 2 == 0:
            e = l // 2
            lam_init = 0.8 - 0.6 * math.exp(-0.3 * l)
            prm = _layer_params(W, e)
            (k, v, qkv, s_new, nkk, d, b, kt, r, vb, g, bonus) = _inproj(
                x, gn, W["w_in_bf"][e], shift_in[e], prm, tm=tm_in, stride=stride)
            lq = W["diff_lambda"][e]
            sub = W["diff_subln"][e][None]
            if prompt:
                oa = _pattn(qkv, lq, sub, lam_init, tq=512).reshape(N, A_WIDTH)
                o, st = _wkv_chunked(nkk, d, b, kt, r, vb)
                o = o.reshape(N, B_WIDTH)
            else:
                nt = T // Bn
                bh = lambda a: a.reshape(nt, Bn, A_HEADS, 2 * A_HD).transpose(1, 2, 0, 3)
                padr = lambda a, rows: jnp.pad(a, ((0, 0), (0, 0), (0, rows - a.shape[2]), (0, 0)))
                q8 = padr(bh(qkv[0, :, :A_WIDTH]), SROWS // 2)
                comp = (jnp.arange(2 * A_HD) // A_HD)
                qh = jnp.concatenate([jnp.where(comp == c, q8, jnp.zeros((), BF16)) for c in range(2)], axis=2)
                oa = _sattn(attn_ctx["page_table"], qh, padr(bh(k[0]), NEW_KEY_ROWS), padr(bh(v[0]), NEW_KEY_ROWS),
                            lq, sub, attn_ctx["cache_k"], attn_ctx["cache_v"], e, nt, lam_init)
                oa = oa[:, :nt].transpose(1, 0, 2).reshape(N, A_WIDTH).astype(BF16)
                streams = [_to_wkv_sample(a[0], Bn) for a in (nkk, d, b, kt, r, vb)]
                o, st = _wkv(*streams, _state_to_wkv_sample(wkv_in[e]), tb=nt)
                o = _o_from_wkv_sample(o, Bn)
                st = _state_from_wkv_sample(st, Bn)
            x = _mixout(x.reshape(N, D_MODEL), oa, o, g.reshape(N, B_WIDTH), bonus.reshape(N, B_WIDTH),
                        W["rwkv_gn_w"][e][None], W["rwkv_gn_b"][e][None], prm["avg"], W["w_out_bf"][e],
                        tm=tm_flat).reshape(G, T, D_MODEL)
            k_rows.append(k)
            v_rows.append(v)
            wkv_out.append(st)
            shift_out.append(s_new)
        else:
            o_ = l // 2
            x, buf = _pool(x, gn, W["pool_w_bf"][o_], W["pool_scale"][o_][None], pool_in[o_],
                           tm=tm_pool, stride=stride, pos0=pos0)
            pool_out.append(buf)
        x, cbuf = _ffn(x, W["norm_ffn"][l][None], W["ffn_up_bf"][l], W["ffn_conv_w"][l],
                       W["ffn_conv_b"][l][None], W["ffn_down_bf"][l], conv_in[l], tm=tm_ffn, stride=stride,
                       out_gain=W["norm_final"][None] if l == DEPTH - 1 else None)
        conv_out.append(cbuf)
    return (x, jnp.stack(k_rows), jnp.stack(v_rows), jnp.stack(wkv_out), jnp.stack(shift_out),
            jnp.stack(pool_out), jnp.stack(conv_out))


def kernel(x_prompt, x_sample, cache_k, cache_v, state_wkv, state_shift, state_pool, state_ffn_conv,
           page_table, norm_mix, norm_ffn, norm_final, w_in, w_out, diff_lambda, diff_subln,
           rwkv_mu, rwkv_w0, rwkv_w2, rwkv_a0, rwkv_a2, rwkv_g2, rwkv_k_k, rwkv_k_a, rwkv_r_k,
           rwkv_gn_w, rwkv_gn_b, pool_w, pool_scale, ffn_up, ffn_conv_w, ffn_conv_b, ffn_down):
    W = dict(norm_mix=norm_mix, norm_ffn=norm_ffn, norm_final=norm_final,
             diff_lambda=diff_lambda, diff_subln=diff_subln, rwkv_mu=rwkv_mu, rwkv_w0=rwkv_w0,
             rwkv_w2=rwkv_w2, rwkv_a0=rwkv_a0, rwkv_a2=rwkv_a2, rwkv_g2=rwkv_g2, rwkv_k_k=rwkv_k_k,
             rwkv_k_a=rwkv_k_a, rwkv_r_k=rwkv_r_k, rwkv_gn_w=rwkv_gn_w, rwkv_gn_b=rwkv_gn_b,
             pool_scale=pool_scale, ffn_conv_w=ffn_conv_w, ffn_conv_b=ffn_conv_b,
             w_in_bf=w_in.astype(BF16), w_out_bf=w_out.astype(BF16), pool_w_bf=pool_w.astype(BF16),
             ffn_up_bf=ffn_up.astype(BF16), ffn_down_bf=ffn_down.astype(BF16))
    n_even, n_odd = state_wkv.shape[0], state_pool.shape[0]
    Bp, Sp, _ = x_prompt.shape
    Bs, Ts, _ = x_sample.shape
    n_pages = page_table.shape[1]
    past_len = n_pages * cache_k.shape[2]

    y_p, k_p, v_p, wkv_p, shift_p, pool_p, conv_p = _trunk(
        x_prompt, W, prompt=True, wkv_in=None,
        shift_in=jnp.zeros((n_even, Bp, 1, B_COLS), F32),
        pool_in=jnp.zeros((n_odd, Bp, POOL_BUF, D_MODEL), F32),
        conv_in=jnp.zeros((DEPTH, Bp, CONV_W - 1, D_FF), F32), attn_ctx=None)
    k_p = k_p.reshape(n_even, Bp, Sp, A_HEADS, 2 * A_HD)
    v_p = v_p.reshape(n_even, Bp, Sp, A_HEADS, 2 * A_HD)
    shift_p = shift_p.reshape(n_even, Bp, B_COLS)

    tmaj = lambda a: jnp.swapaxes(a, -3, -2)
    flat = lambda a: a.reshape(a.shape[:-3] + (1, a.shape[-3] * a.shape[-2], a.shape[-1]))
    ctx = dict(batch=Bs, past_len=past_len, page_table=page_table,
               cache_k=cache_k.reshape(cache_k.shape[:2] + (PAGE_SIZE * A_HEADS, 2 * A_HD)),
               cache_v=cache_v.reshape(cache_v.shape[:2] + (PAGE_SIZE * A_HEADS, 2 * A_HD)))
    y_s, k_s, v_s, wkv_s, shift_s, pool_s, conv_s = _trunk(
        flat(tmaj(x_sample)), W, prompt=False, wkv_in=state_wkv,
        shift_in=state_shift[:, None], pool_in=flat(tmaj(state_pool)),
        conv_in=flat(tmaj(state_ffn_conv)), attn_ctx=ctx)
    unflat = lambda a, j: tmaj(a.reshape(a.shape[:-3] + (j, Bs, a.shape[-1])))
    y_s = unflat(y_s, Ts)
    k_s = k_s.reshape(n_even, Ts, Bs, A_HEADS, 2 * A_HD).transpose(0, 2, 1, 3, 4)
    v_s = v_s.reshape(n_even, Ts, Bs, A_HEADS, 2 * A_HD).transpose(0, 2, 1, 3, 4)
    shift_s = shift_s.reshape(n_even, Bs, B_COLS)
    pool_s = unflat(pool_s, POOL_BUF)
    conv_s = unflat(conv_s, CONV_W - 1)
    return (y_p, y_s, k_p, v_p, wkv_p, shift_p, pool_p, conv_p,
            k_s, v_s, wkv_s, shift_s, pool_s, conv_s)
```

```python
import functools
import math

import jax
import jax.numpy as jnp
from jax import lax
from jax.experimental import pallas as pl
from jax.experimental.pallas import tpu as pltpu

F32 = jnp.float32
BF16 = jnp.bfloat16

D_MODEL = 1024
DEPTH = 4
PAGE_SIZE = 128
A_HD = 64
A_HEADS = 4
A_WIDTH = 512
A_COLS = 3 * A_WIDTH
A_SCALE = A_HD ** -0.5
Q_SCALE = A_SCALE * math.log2(math.e)
B_HD = 64
B_WIDTH = 512
B_HEADS = 8
DECAY_RANK = 64
ICLR_RANK = 64
GATE_RANK = 128
B_COLS = 3 * B_WIDTH + DECAY_RANK + ICLR_RANK + GATE_RANK
IN_COLS = A_COLS + B_COLS
POOL_WINDOWS = (2, 4, 8, 16)
POOL_GW = D_MODEL // len(POOL_WINDOWS)
POOL_BUF = max(POOL_WINDOWS) - 1
D_FF = 2816
CONV_W = 3
NORM_EPS = 1e-6
SUBLN_EPS = 1e-5
GN_EPS = 64e-5

LANES = 128
SUBLANES = 8
VMEM_LIMIT = 56 * 1024 * 1024
ROW_TILE = 512
ATTN_TILE = 1024
FF_CHUNK = 1408
PAGES_PER_STEP = 16
NEW_KEY_ROWS = 16
SROWS = 16
HEAD_LOG2 = 6


def _div_nonneg(x, n):
    if n == 1:
        return x
    if n & (n - 1) == 0:
        return lax.shift_right_logical(x, n.bit_length() - 1)
    return x // n


def _cparams(sem):
    return pltpu.CompilerParams(dimension_semantics=sem, vmem_limit_bytes=VMEM_LIMIT)


def _const_spec(shape):
    nd = len(shape)
    return pl.BlockSpec(shape, lambda *_: (0,) * nd, pipeline_mode=pl.Buffered(1))


def _layer_spec(shape, li):
    nd = len(shape)
    return pl.BlockSpec((None,) + tuple(shape), lambda *_: (li,) + (0,) * nd, pipeline_mode=pl.Buffered(1))


def _rms(x, g, eps):
    return x * lax.rsqrt(jnp.mean(x * x, axis=-1, keepdims=True) + eps) * g


def _bf16_dot(x, m):
    return jnp.dot(x.astype(BF16), m, preferred_element_type=F32)


def _lane_tile(x, n):
    return jnp.concatenate([x] * n, axis=1)


def _softplus(z):
    return jnp.maximum(z, 0.0) + jnp.log1p(jnp.exp(-jnp.abs(z)))


def _inproj_kernel(x_ref, gn_ref, win_ref, sprev_ref, mu_ref, w0_ref, w2p_ref, a0_ref, a2p_ref,
                   g2_ref, kk_ref, ka_ref, rk_ref, seg_ref,
                   k_ref, v_ref, qkv_ref, sout_ref, nkk_ref, d_ref, b_ref, kt_ref, r_ref,
                   vb_ref, g_ref, bonus_ref, ext_ref, *, tm, stride, halo):
    s = pl.program_id(1)
    ns = pl.num_programs(1)
    h = _rms(x_ref[...], gn_ref[...], NORM_EPS).astype(BF16)
    p = jnp.dot(h, win_ref[...], preferred_element_type=F32)
    for hh in range(A_HEADS):
        c0 = A_WIDTH + hh * 2 * A_HD
        k_ref[pl.ds(hh, tm, stride=A_HEADS), :] = p[:, c0:c0 + 2 * A_HD]
        v_ref[pl.ds(hh, tm, stride=A_HEADS), :] = p[:, A_WIDTH + c0:A_WIDTH + c0 + 2 * A_HD]
    qkv_ref[:, :A_WIDTH] = (p[:, :A_WIDTH] * Q_SCALE).astype(BF16)
    qkv_ref[:, A_WIDTH:] = p[:, A_WIDTH:A_COLS].astype(BF16)
    pb = p[:, A_COLS:]

    @pl.when(s == 0)
    def _():
        ext_ref[halo - stride:halo, :] = sprev_ref[...]

    ext_ref[halo:halo + tm, :] = pb
    prev = ext_ref[halo - stride:halo - stride + tm, :]
    ext_ref[halo - stride:halo, :] = ext_ref[halo + tm - stride:halo + tm, :]

    @pl.when(s == ns - 1)
    def _():
        sout_ref[...] = ext_ref[halo + tm - stride:halo + tm, :]

    xm = pb + (prev - pb) * mu_ref[...]
    r = xm[:, :B_WIDTH]
    k = xm[:, B_WIDTH:2 * B_WIDTH]
    v = xm[:, 2 * B_WIDTH:3 * B_WIDTH]
    wa = xm[:, 3 * B_WIDTH:3 * B_WIDTH + DECAY_RANK + ICLR_RANK]
    gd = xm[:, 3 * B_WIDTH + DECAY_RANK + ICLR_RANK:]
    wlin = w0_ref[...] + jnp.dot(jnp.tanh(wa).astype(BF16), w2p_ref[...], preferred_element_type=F32)
    w = -_softplus(-wlin) - 0.5
    log_decay = -jnp.exp(w)
    a = jax.nn.sigmoid(a0_ref[...] + jnp.dot(wa.astype(BF16), a2p_ref[...], preferred_element_type=F32))
    g = jnp.dot(jax.nn.sigmoid(gd).astype(BF16), g2_ref[...], preferred_element_type=F32)
    kk = k * kk_ref[...]
    ss = _bf16_dot(kk * kk, seg_ref[...])
    kk = kk / jnp.maximum(jnp.sqrt(ss), 1e-12)
    kt = k * (1.0 + (a - 1.0) * ka_ref[...])
    bonus = _bf16_dot(r * kt * rk_ref[...], seg_ref[...]) * v
    nkk_ref[...] = -kk
    d_ref[...] = log_decay
    b_ref[...] = kk * a
    kt_ref[...] = kt
    r_ref[...] = r
    vb_ref[...] = v
    g_ref[...] = g
    bonus_ref[...] = bonus


def _inproj(x, gn, win, li, sprev, prm, *, tm, stride):
    G, T, _ = x.shape
    halo = max(SUBLANES, stride)
    ns = T // tm
    row = lambda w: pl.BlockSpec((None, tm, w), lambda g, s: (g, s, 0))
    stream = jax.ShapeDtypeStruct((G, T, B_WIDTH), F32)
    in_specs = [row(D_MODEL), _const_spec((1, D_MODEL)), _layer_spec((D_MODEL, IN_COLS), li),
                pl.BlockSpec((None, stride, B_COLS), lambda g, s: (g, 0, 0)),
                _const_spec((1, B_COLS)), _const_spec((1, B_WIDTH)), _const_spec((LANES, B_WIDTH)),
                _const_spec((1, B_WIDTH)), _const_spec((LANES, B_WIDTH)), _const_spec((GATE_RANK, B_WIDTH)),
                _const_spec((1, B_WIDTH)), _const_spec((1, B_WIDTH)), _const_spec((1, B_WIDTH)),
                _const_spec((B_WIDTH, B_WIDTH))]
    kv_shape = jax.ShapeDtypeStruct((G, T * A_HEADS, 2 * A_HD), F32)
    kv_spec = pl.BlockSpec((None, tm * A_HEADS, 2 * A_HD), lambda g, s: (g, s, 0))
    out_shape = [kv_shape, kv_shape, jax.ShapeDtypeStruct((G, T, A_COLS), BF16),
                 jax.ShapeDtypeStruct((G, stride, B_COLS), F32)] + [stream] * 8
    out_specs = [kv_spec, kv_spec, row(A_COLS),
                 pl.BlockSpec((None, stride, B_COLS), lambda g, s: (g, 0, 0))] + [row(B_WIDTH)] * 8
    return pl.pallas_call(
        functools.partial(_inproj_kernel, tm=tm, stride=stride, halo=halo),
        grid=(G, ns), in_specs=in_specs, out_specs=out_specs, out_shape=out_shape,
        scratch_shapes=[pltpu.VMEM((halo + tm, B_COLS), F32)],
        compiler_params=_cparams(("arbitrary", "arbitrary")), name="inproj",
    )(x, gn, win, sprev, prm["mu"], prm["w0"], prm["w2p"], prm["a0"], prm["a2p"], prm["g2"],
      prm["k_k"], prm["k_a"], prm["r_k"], prm["seg"])


def _lambda_full(lq, lam_init):
    s01 = jnp.sum(lq[0:1, :] * lq[1:2, :], axis=-1, keepdims=True)
    s23 = jnp.sum(lq[2:3, :] * lq[3:4, :], axis=-1, keepdims=True)
    return jnp.exp(s01) - jnp.exp(s23) + lam_init


def _pattn_kernel(qi_ref, kj_ref, q_ref, k_ref, v_ref, lq_ref, sub_ref, o_ref,
                  m0_ref, l0_ref, a0_ref, m1_ref, l1_ref, a1_ref, *, tq, lam_init):
    qi = qi_ref[pl.program_id(2)]
    kj = kj_ref[pl.program_id(2)]

    @pl.when(kj == 0)
    def _():
        for m_ref, l_ref, a_ref in ((m0_ref, l0_ref, a0_ref), (m1_ref, l1_ref, a1_ref)):
            m_ref[...] = jnp.full(m_ref.shape, -jnp.inf, F32)
            l_ref[...] = jnp.zeros(l_ref.shape, F32)
            a_ref[...] = jnp.zeros(a_ref.shape, F32)

    def accumulate(diagonal):
        q = q_ref[...]
        k = k_ref[...]
        v = v_ref[...]
        lane = lax.broadcasted_iota(jnp.int32, q.shape, 1)
        zero = jnp.zeros_like(q)
        if diagonal:
            keep = (lax.broadcasted_iota(jnp.int32, (tq, tq), 1)
                    <= lax.broadcasted_iota(jnp.int32, (tq, tq), 0))
        refs = ((m0_ref, l0_ref, a0_ref), (m1_ref, l1_ref, a1_ref))
        comps = range(2)
        qc = [jnp.where((lane >= c * A_HD) & (lane < (c + 1) * A_HD), q, zero) for c in comps]
        s = [lax.dot_general(qc[c], k, (((1,), (1,)), ((), ())), preferred_element_type=F32) for c in comps]
        if diagonal:
            s = [jnp.where(keep, s[c], -jnp.inf) for c in comps]
        m_old = [refs[c][0][...] for c in comps]
        m_new = [jnp.maximum(m_old[c], jnp.max(s[c], axis=-1, keepdims=True)) for c in comps]
        alpha = [jnp.exp2(m_old[c] - m_new[c]) for c in comps]
        p = [jnp.exp2(s[c] - _lane_tile(m_new[c], tq // LANES)) for c in comps]
        for c in comps:
            m_ref, l_ref, a_ref = refs[c]
            l_ref[...] = alpha[c] * l_ref[...] + jnp.sum(p[c], axis=-1, keepdims=True)
            a_ref[...] = alpha[c] * a_ref[...] + jnp.dot(p[c].astype(BF16), v, preferred_element_type=F32)
            m_ref[...] = m_new[c]

    @pl.when(kj < qi)
    def _():
        accumulate(False)

    @pl.when(kj == qi)
    def _():
        accumulate(True)
        lam = _lambda_full(lq_ref[...], lam_init)
        o = a0_ref[...] / l0_ref[...] - lam * (a1_ref[...] / l1_ref[...])
        o = _rms(o, sub_ref[...], SUBLN_EPS) * (1.0 - lam_init)
        o_ref[...] = o.astype(o_ref.dtype)


def _pattn(qkv, lq, sub, lam_init, *, tq):
    B, S, _ = qkv.shape
    nq = S // tq
    nh = A_HEADS
    pairs = [(i, j) for i in range(nq) for j in range(i + 1)]
    qi_tab = jnp.asarray([p[0] for p in pairs], jnp.int32)
    kj_tab = jnp.asarray([p[1] for p in pairs], jnp.int32)
    q_spec = pl.BlockSpec((None, tq, LANES), lambda b, h, n, qt, kt: (b, qt[n], h))
    k_spec = pl.BlockSpec((None, tq, LANES), lambda b, h, n, qt, kt: (b, kt[n], nh + h))
    v_spec = pl.BlockSpec((None, tq, LANES), lambda b, h, n, qt, kt: (b, kt[n], 2 * nh + h))
    grid_spec = pltpu.PrefetchScalarGridSpec(
        num_scalar_prefetch=2, grid=(B, nh, len(pairs)),
        in_specs=[q_spec, k_spec, v_spec,
                  pl.BlockSpec((4, A_HD), lambda b, h, n, qt, kt: (0, 0)),
                  pl.BlockSpec((1, LANES), lambda b, h, n, qt, kt: (0, 0))],
        out_specs=pl.BlockSpec((None, tq, LANES), lambda b, h, n, qt, kt: (b, qt[n], h)),
        scratch_shapes=[pltpu.VMEM((tq, LANES), F32)] * 6)
    return pl.pallas_call(
        functools.partial(_pattn_kernel, tq=tq, lam_init=lam_init),
        grid_spec=grid_spec, out_shape=jax.ShapeDtypeStruct((B, S, A_WIDTH), BF16),
        compiler_params=_cparams(("arbitrary",) * 3), name="pattn",
    )(qi_tab, kj_tab, qkv, qkv, qkv, lq, sub)


def _sattn_kernel(pt_ref, q_ref, kn_ref, vn_ref, lq_ref, sub_ref, *rest, npg, nt, lam_init):
    k_refs = rest[:npg]
    v_refs = rest[npg:2 * npg]
    o_ref = rest[2 * npg]
    m_ref, l_ref, acc_ref = rest[2 * npg + 1:]
    j = pl.program_id(1)
    nj = pl.num_programs(1)

    @pl.when(j == 0)
    def _():
        m_ref[...] = jnp.full(m_ref.shape, -jnp.inf, F32)
        l_ref[...] = jnp.zeros(l_ref.shape, F32)
        acc_ref[...] = jnp.zeros(acc_ref.shape, F32)

    nt_dims = (((1,), (1,)), ((), ()))
    qs = [q_ref[h] for h in range(A_HEADS)]

    def head_rows(pg_ref, h):
        return pg_ref[pl.ds(h, PAGE_SIZE, stride=A_HEADS), :].astype(BF16)

    def update(s, pv):
        m_old = m_ref[...]
        m_new = jnp.maximum(m_old, jnp.max(s, axis=-1, keepdims=True))
        alpha = jnp.exp2(m_old - m_new)
        p = jnp.exp2(s - (_lane_tile(m_new, s.shape[1] // LANES) if s.shape[1] > LANES
                          else m_new[:, :s.shape[1]]))
        l_ref[...] = alpha * l_ref[...] + jnp.sum(p, axis=-1, keepdims=True)
        acc_ref[...] = alpha * acc_ref[...] + pv(p.astype(BF16))
        m_ref[...] = m_new

    s_heads = [jnp.concatenate(
        [lax.dot_general(qs[h], head_rows(kr, h), nt_dims, preferred_element_type=F32) for kr in k_refs],
        axis=1) for h in range(A_HEADS)]

    def pv_pages(p):
        outs = []
        for h in range(A_HEADS):
            ph = p[h * SROWS:(h + 1) * SROWS, :]
            tot = None
            for i, vr in enumerate(v_refs):
                part = jnp.dot(ph[:, i * PAGE_SIZE:(i + 1) * PAGE_SIZE], head_rows(vr, h),
                               preferred_element_type=F32)
                tot = part if tot is None else tot + part
            outs.append(tot)
        return jnp.concatenate(outs, axis=0)

    update(jnp.concatenate(s_heads, axis=0), pv_pages)

    @pl.when(j == nj - 1)
    def _():
        s_new = jnp.concatenate(
            [lax.dot_general(qs[h], kn_ref[h].astype(BF16), nt_dims, preferred_element_type=F32)
             for h in range(A_HEADS)], axis=0)
        key = lax.broadcasted_iota(jnp.int32, s_new.shape, 1)
        tok = lax.broadcasted_iota(jnp.int32, s_new.shape, 0) & (SROWS // 2 - 1)
        s_new = jnp.where((key <= tok) & (key < nt), s_new, -jnp.inf)

        def pv_new(p):
            return jnp.concatenate(
                [jnp.dot(p[h * SROWS:(h + 1) * SROWS, :], vn_ref[h].astype(BF16), preferred_element_type=F32)
                 for h in range(A_HEADS)], axis=0)

        update(s_new, pv_new)
        lam = _lambda_full(lq_ref[...], lam_init)
        half = SROWS // 2
        outs = []
        for h in range(A_HEADS):
            r0 = h * SROWS
            o = (acc_ref[r0:r0 + half, :] / l_ref[r0:r0 + half, :]
                 - lam * (acc_ref[r0 + half:r0 + SROWS, :] / l_ref[r0 + half:r0 + SROWS, :]))
            outs.append(_rms(o, sub_ref[...], SUBLN_EPS) * (1.0 - lam_init))
        o_ref[...] = jnp.concatenate(outs, axis=-1)


def _sattn(page_table, qh, k_new, v_new, lq, sub, kc, vc, e, nt, lam_init):
    Bn = qh.shape[0]
    n_pages = page_table.shape[1]
    npg = PAGES_PER_STEP
    nj = n_pages // npg
    page_rows = PAGE_SIZE * A_HEADS
    per_b = lambda r: pl.BlockSpec((None, A_HEADS, r, LANES), lambda b, j, pt: (b, 0, 0, 0))

    def page_spec(i):
        return pl.BlockSpec((None, None, page_rows, LANES), lambda b, j, pt: (e, pt[b, j * npg + i], 0, 0))

    in_specs = ([per_b(SROWS), per_b(NEW_KEY_ROWS), per_b(NEW_KEY_ROWS),
                 pl.BlockSpec((4, A_HD), lambda b, j, pt: (0, 0)),
                 pl.BlockSpec((1, LANES), lambda b, j, pt: (0, 0))]
                + [page_spec(i) for i in range(npg)] * 2)
    nr = A_HEADS * SROWS
    grid_spec = pltpu.PrefetchScalarGridSpec(
        num_scalar_prefetch=1, grid=(Bn, nj), in_specs=in_specs,
        out_specs=pl.BlockSpec((None, SROWS // 2, A_WIDTH), lambda b, j, pt: (b, 0, 0)),
        scratch_shapes=[pltpu.VMEM((nr, LANES), F32)] * 3)
    return pl.pallas_call(
        functools.partial(_sattn_kernel, npg=npg, nt=nt, lam_init=lam_init),
        grid_spec=grid_spec, out_shape=jax.ShapeDtypeStruct((Bn, SROWS // 2, A_WIDTH), F32),
        compiler_params=_cparams(("arbitrary", "arbitrary")), name="sattn",
    )(page_table, qh, k_new, v_new, lq, sub, *([kc] * npg), *([vc] * npg))


def _wkv_kernel(nkk_ref, d_ref, b_ref, kt_ref, r_ref, v_ref, s0_ref, o_ref, st_ref, s_scr, *, tb, nv):
    t0 = pl.program_id(1)

    @pl.when(t0 == 0)
    def _():
        s_scr[...] = s0_ref[...]

    def step(t, carry):
        nkk = nkk_ref[t]
        d = jnp.exp(d_ref[t])
        b = b_ref[t]
        kt = kt_ref[t]
        r = r_ref[t]
        dr = d * r
        br = jnp.sum(b * r, axis=0, keepdims=True)
        kr = jnp.sum(kt * r, axis=0, keepdims=True)

        def row(vr, c):
            S = s_scr[vr]
            sa = jnp.sum(S * nkk, axis=0, keepdims=True)
            qo = jnp.sum(S * dr, axis=0, keepdims=True)
            vv = v_ref[t, pl.ds(vr, 1), :]
            s_scr[vr] = S * d + sa * b + vv * kt
            o_ref[t, pl.ds(vr, 1), :] = qo + sa * br + vv * kr
            return c

        return lax.fori_loop(0, nv, row, carry, unroll=2)

    lax.fori_loop(0, tb, step, 0)
    st_ref[...] = s_scr[...]


def _wkv(nkk, d, b, kt, r, v, s0, *, tb):
    LG, T, _, _ = nkk.shape
    nv = v.shape[2]
    kspec = pl.BlockSpec((None, tb, B_HD, LANES), lambda g, t: (g, t, 0, 0))
    vspec = pl.BlockSpec((None, tb, nv, LANES), lambda g, t: (g, t, 0, 0))
    sspec = pl.BlockSpec((None, nv, B_HD, LANES), lambda g, t: (g, 0, 0, 0))
    return pl.pallas_call(
        functools.partial(_wkv_kernel, tb=tb, nv=nv),
        grid=(LG, T // tb), in_specs=[kspec] * 5 + [vspec, sspec],
        out_specs=[vspec, sspec],
        out_shape=[jax.ShapeDtypeStruct((LG, T, nv, LANES), F32),
                   jax.ShapeDtypeStruct((LG, nv, B_HD, LANES), F32)],
        scratch_shapes=[pltpu.VMEM((nv, B_HD, LANES), F32)],
        compiler_params=_cparams(("arbitrary", "arbitrary")), name="wkv",
    )(nkk, d, b, kt, r, v, s0)


WKV_CHUNK = 64
WKV_GROUP = 4


def _split3_dot(m, x):
    x1 = x.astype(BF16)
    r1 = x - x1.astype(F32)
    x2 = r1.astype(BF16)
    x3 = (r1 - x2.astype(F32)).astype(BF16)
    return (jnp.dot(m, x1, preferred_element_type=F32) + jnp.dot(m, x2, preferred_element_type=F32)
            + jnp.dot(m, x3, preferred_element_type=F32))


def _wkv_chunk_math(a, ld, b, kt, r, v, s):
    L, HW = WKV_CHUNK, WKV_GROUP * B_HD
    nt_dims = (((1,), (1,)), ((), ()))
    tn_dims = (((0,), (0,)), ((), ()))
    row = lax.broadcasted_iota(jnp.int32, (L, HW), 0)
    col = lax.broadcasted_iota(jnp.int32, (L, HW), 1) & (B_HD - 1)
    strict = col < row
    incl = col <= row
    blk = lax.shift_right_logical
    bd_mask = (lax.shift_right_logical(lax.broadcasted_iota(jnp.int32, (HW, HW), 0), HEAD_LOG2)
               == lax.shift_right_logical(lax.broadcasted_iota(jnp.int32, (HW, HW), 1), HEAD_LOG2))

    def bd(y):
        y = y.astype(BF16)
        return jnp.where(bd_mask, jnp.concatenate([y] * WKV_GROUP, axis=0), jnp.zeros((), BF16))

    def hprod(x, y):
        return jnp.dot(x.astype(BF16), bd(y), preferred_element_type=F32)

    def each(f, *lists):
        return [f(*xs) for xs in zip(*lists)]

    def dotg(dims):
        return lambda x, y: lax.dot_general(x, y, dims, preferred_element_type=F32)

    tri = (lax.broadcasted_iota(jnp.int32, (L, L), 1) <= lax.broadcasted_iota(jnp.int32, (L, L), 0)).astype(BF16)
    logp = each(lambda x: _split3_dot(tri, x), ld)
    ep = each(jnp.exp, logp)
    p_last = each(lambda x: x[L - 1:L, :], ep)
    en = each(lambda x: jnp.exp(-x), logp)
    at = each(lambda x, lp, l_: x * jnp.exp(lp - l_), a, logp, ld)
    rt = each(jnp.multiply, r, ep)
    bt = each(jnp.multiply, b, en)
    ktt = each(jnp.multiply, kt, en)
    bh = each(lambda x, pp: (x * pp).astype(BF16), bt, p_last)
    kh = each(lambda x, pp: (x * pp).astype(BF16), ktt, p_last)

    lhs = each(lambda x, y: jnp.concatenate([x, y], axis=0).astype(BF16), at, rt)
    xb = each(lambda x, y: dotg(nt_dims)(x, bd(y)), lhs, bt)
    xk = each(lambda x, y: dotg(nt_dims)(x, bd(y)), lhs, ktt)
    n = each(lambda x: jnp.where(strict, x[:L], 0.0), xb)
    arb = each(lambda x: jnp.where(incl, x[L:], 0.0), xb)
    aak = each(lambda x: jnp.where(strict, x[:L], 0.0), xk)
    ark = each(lambda x: jnp.where(incl, x[L:], 0.0), xk)

    eye = jnp.where(col == row, 1.0, 0.0)
    tm = each(lambda x: eye + jnp.where(blk(row, 1) == blk(col, 1), x, 0.0), n)
    for lg in range(1, HEAD_LOG2):
        sel = (blk(row, lg + 1) == blk(col, lg + 1)) & (blk(row, lg) != blk(col, lg))
        w = each(lambda t_, x: hprod(t_, jnp.where(sel, x, 0.0)), tm, n)
        tm = each(lambda t_, w_: t_ + hprod(w_, t_), tm, w)

    rloc = each(hprod, aak, v)
    uloc = each(hprod, tm, rloc)
    ah = each(hprod, tm, at)
    ro = each(lambda x, y, z: (x + hprod(y, z)).astype(BF16), rt, arb, ah)
    oc = each(lambda x, u, y, vv: hprod(x, u) + hprod(y, vv), arb, uloc, ark, v)
    mpp = each(lambda x, y: jnp.where(bd_mask, dotg(tn_dims)(x.astype(BF16), y), 0.0).astype(BF16), ah, bh)
    sc = each(lambda u, y, vv, z: jnp.where(
        bd_mask, dotg(tn_dims)(u.astype(BF16), y) + dotg(tn_dims)(vv.astype(BF16), z), 0.0), uloc, bh, v, kh)

    s_hi = each(lambda x: x.astype(BF16), s)
    s_lo = each(lambda x, h: (x - h.astype(F32)).astype(BF16), s, s_hi)
    o = each(lambda q, h, l_, c_: dotg(nt_dims)(q, h) + dotg(nt_dims)(q, l_) + c_, ro, s_hi, s_lo, oc)
    s_new = each(lambda x, pp, h, l_, m_, c_: (x * pp + jnp.dot(h, m_, preferred_element_type=F32)
                                               + jnp.dot(l_, m_, preferred_element_type=F32) + c_),
                 s, p_last, s_hi, s_lo, mpp, sc)
    return o, s_new


def _wkv_chunk_kernel(a_ref, ld_ref, b_ref, kt_ref, r_ref, v_ref, o_ref, st_ref, s_scr, *, nb):
    c = pl.program_id(1)
    nc = pl.num_programs(1)
    HW = WKV_GROUP * B_HD

    @pl.when(c == 0)
    def _():
        s_scr[...] = jnp.zeros(s_scr.shape, F32)

    groups = [(bb, g) for bb in range(nb) for g in range(B_WIDTH // HW)]
    sls = [(bb, slice(None), slice(g * HW, (g + 1) * HW)) for bb, g in groups]
    o, s_new = _wkv_chunk_math(*[[ref[sl] for sl in sls] for ref in (a_ref, ld_ref, b_ref, kt_ref, r_ref, v_ref)],
                               [s_scr[bb, g] for bb, g in groups])
    for sl, (bb, g), o_i, s_i in zip(sls, groups, o, s_new):
        o_ref[sl] = o_i
        s_scr[bb, g] = s_i

    @pl.when(c == nc - 1)
    def _():
        st_ref[...] = s_scr[...]


def _wkv_chunked(a, ld, b, kt, r, v, *, nb=4):
    Bn, T, _ = a.shape
    L, HW = WKV_CHUNK, WKV_GROUP * B_HD
    ng = B_WIDTH // HW
    nb = min(nb, Bn)
    assert Bn % nb == 0 and T % L == 0
    spec = pl.BlockSpec((nb, L, B_WIDTH), lambda bb, c: (bb, c, 0))
    st_spec = pl.BlockSpec((nb, ng, HW, HW), lambda bb, c: (bb, 0, 0, 0))
    o, st = pl.pallas_call(
        functools.partial(_wkv_chunk_kernel, nb=nb),
        grid=(Bn // nb, T // L), in_specs=[spec] * 6, out_specs=[spec, st_spec],
        out_shape=[jax.ShapeDtypeStruct((Bn, T, B_WIDTH), F32), jax.ShapeDtypeStruct((Bn, ng, HW, HW), F32)],
        scratch_shapes=[pltpu.VMEM((nb, ng, HW, HW), F32)],
        compiler_params=_cparams(("arbitrary",) * 2), name="wkv_chunked",
    )(a, ld, b, kt, r, v)
    st = st.reshape(Bn, ng, WKV_GROUP, B_HD, WKV_GROUP, B_HD)
    st = jnp.stack([st[:, :, i, :, i, :] for i in range(WKV_GROUP)], axis=2)
    return o, st.reshape(Bn, B_HEADS, B_HD, B_HD)


def _mixout_kernel(x_ref, oa_ref, o_ref, g_ref, bonus_ref, gnw_ref, gnb_ref, avg_ref, wout_ref, xo_ref):
    o = o_ref[...]
    mu = _bf16_dot(o, avg_ref[...])
    dlt = o - mu
    var = _bf16_dot(dlt * dlt, avg_ref[...])
    on = dlt * lax.rsqrt(var + GN_EPS) * gnw_ref[...] + gnb_ref[...]
    ob = ((on + bonus_ref[...]) * g_ref[...]).astype(BF16)
    y = (jnp.dot(oa_ref[...], wout_ref[0:A_WIDTH, :], preferred_element_type=F32)
         + jnp.dot(ob, wout_ref[A_WIDTH:, :], preferred_element_type=F32))
    xo_ref[...] = x_ref[...] + y


def _mixout(x, oa, o, g, bonus, gnw, gnb, avg, wout, li, *, tm):
    N = x.shape[0]
    row = lambda w: pl.BlockSpec((tm, w), lambda i: (i, 0))
    return pl.pallas_call(
        _mixout_kernel, grid=(N // tm,),
        in_specs=[row(D_MODEL), row(A_WIDTH), row(B_WIDTH), row(B_WIDTH), row(B_WIDTH),
                  _const_spec((1, B_WIDTH)), _const_spec((1, B_WIDTH)),
                  _const_spec((B_WIDTH, B_WIDTH)), _layer_spec((D_MODEL, D_MODEL), li)],
        out_specs=row(D_MODEL), out_shape=jax.ShapeDtypeStruct((N, D_MODEL), F32),
        compiler_params=_cparams(("arbitrary",)), name="mixout",
    )(x, oa, o, g, bonus, gnw, gnb, avg, wout)


def _pool_kernel(x_ref, gn_ref, pw_ref, ps_ref, pprev_ref, xo_ref, pout_ref, ext_ref,
                 *, tm, stride, halo, pos0):
    s = pl.program_id(1)
    ns = pl.num_programs(1)
    keep = POOL_BUF * stride
    x = x_ref[...]
    h = _rms(x, gn_ref[...], NORM_EPS)

    @pl.when(s == 0)
    def _():
        ext_ref[halo - keep:halo, :] = pprev_ref[...]

    ext_ref[halo:halo + tm, :] = h
    t = _div_nonneg(s * tm + lax.broadcasted_iota(jnp.int32, (tm, 1), 0), stride)
    ys = []
    for gi, win in enumerate(POOL_WINDOWS):
        c0, c1 = gi * POOL_GW, (gi + 1) * POOL_GW
        cur = ext_ref[halo:halo + tm, c0:c1]
        wsum = cur
        for jj in range(1, win):
            wsum = wsum + ext_ref[halo - jj * stride:halo - jj * stride + tm, c0:c1]
        cnt = jnp.minimum(pos0 + t + 1, win).astype(F32)
        m = wsum / cnt - cur
        ys.append(jnp.dot(m.astype(BF16), pw_ref[gi], preferred_element_type=F32))
    xo_ref[...] = x + jnp.concatenate(ys, axis=-1) * ps_ref[...]

    @pl.when(s == ns - 1)
    def _():
        pout_ref[...] = ext_ref[halo + tm - keep:halo + tm, :]

    if tm >= keep:
        ext_ref[halo - keep:halo, :] = ext_ref[halo + tm - keep:halo + tm, :]


def _pool(x, gn, pw, ps, pprev, *, tm, stride, pos0):
    G, T, _ = x.shape
    halo = 2 * SUBLANES * stride
    keep = POOL_BUF * stride
    assert T // tm == 1 or tm >= keep
    row = pl.BlockSpec((None, tm, D_MODEL), lambda g, s: (g, s, 0))
    st = pl.BlockSpec((None, keep, D_MODEL), lambda g, s: (g, 0, 0))
    return pl.pallas_call(
        functools.partial(_pool_kernel, tm=tm, stride=stride, halo=halo, pos0=pos0),
        grid=(G, T // tm),
        in_specs=[row, _const_spec((1, D_MODEL)), _const_spec((len(POOL_WINDOWS), POOL_GW, POOL_GW)),
                  _const_spec((1, D_MODEL)), st],
        out_specs=[row, st],
        out_shape=[jax.ShapeDtypeStruct((G, T, D_MODEL), F32), jax.ShapeDtypeStruct((G, keep, D_MODEL), F32)],
        scratch_shapes=[pltpu.VMEM((halo + tm, D_MODEL), F32)],
        compiler_params=_cparams(("arbitrary", "arbitrary")), name="pool",
    )(x, gn, pw, ps, pprev)


def _ffn_kernel(x_ref, gn_ref, wup_ref, cw_ref, cb_ref, wdn_ref, cprev_ref, *rest, tm, stride, halo, out_norm):
    gout_ref = rest[0] if out_norm else None
    xo_ref, cout_ref, ext_ref = rest[-3:]
    s = pl.program_id(1)
    ns = pl.num_programs(1)
    keep = (CONV_W - 1) * stride
    x = x_ref[...]
    h = _rms(x, gn_ref[...], NORM_EPS).astype(BF16)

    @pl.when(s == 0)
    def _():
        ext_ref[halo - keep:halo, :] = cprev_ref[...]

    def up(c):
        c0, c1 = c * FF_CHUNK, (c + 1) * FF_CHUNK
        gate = jnp.dot(h, wup_ref[:, c0:c1], preferred_element_type=F32)
        val = jnp.dot(h, wup_ref[:, D_FF + c0:D_FF + c1], preferred_element_type=F32)
        ext_ref[halo:halo + tm, c0:c1] = gate
        return gate, val

    def down(c, gate, val):
        c0, c1 = c * FF_CHUNK, (c + 1) * FF_CHUNK
        cc = cb_ref[:, c0:c1]
        for jj in range(CONV_W - 1):
            back = (CONV_W - 1 - jj) * stride
            cc = cc + ext_ref[halo - back:halo - back + tm, c0:c1] * cw_ref[jj:jj + 1, c0:c1]
        cc = cc + gate * cw_ref[CONV_W - 1:CONV_W, c0:c1]
        act = (0.5 * cc * (1.0 + lax.erf(cc * (2.0 ** -0.5))) * val).astype(BF16)
        return jnp.dot(act, wdn_ref[c0:c1, :], preferred_element_type=F32)

    nchunk = D_FF // FF_CHUNK
    acc = x
    pending = up(0)
    for c in range(nchunk):
        nxt = up(c + 1) if c + 1 < nchunk else None
        acc = acc + down(c, *pending)
        pending = nxt
    xo_ref[...] = _rms(acc, gout_ref[...], NORM_EPS) if out_norm else acc

    @pl.when(s == ns - 1)
    def _():
        cout_ref[...] = ext_ref[halo + tm - keep:halo + tm, :]

    ext_ref[halo - keep:halo, :] = ext_ref[halo + tm - keep:halo + tm, :]


def _ffn(x, gn, wup, cw, cb, wdn, li, cprev, *, tm, stride, out_gain=None):
    G, T, _ = x.shape
    keep = (CONV_W - 1) * stride
    halo = max(SUBLANES, keep)
    assert tm >= keep
    row = pl.BlockSpec((None, tm, D_MODEL), lambda g, s: (g, s, 0))
    st = pl.BlockSpec((None, keep, D_FF), lambda g, s: (g, 0, 0))
    out_norm = out_gain is not None
    extra_specs, extra_args = ([_const_spec((1, D_MODEL))], [out_gain]) if out_norm else ([], [])
    return pl.pallas_call(
        functools.partial(_ffn_kernel, tm=tm, stride=stride, halo=halo, out_norm=out_norm),
        grid=(G, T // tm),
        in_specs=[row, _const_spec((1, D_MODEL)), _layer_spec((D_MODEL, 2 * D_FF), li),
                  _const_spec((CONV_W, D_FF)), _const_spec((1, D_FF)), _layer_spec((D_FF, D_MODEL), li), st]
        + extra_specs,
        out_specs=[row, st],
        out_shape=[jax.ShapeDtypeStruct((G, T, D_MODEL), F32), jax.ShapeDtypeStruct((G, keep, D_FF), F32)],
        scratch_shapes=[pltpu.VMEM((halo + tm, D_FF), F32)],
        compiler_params=_cparams(("arbitrary", "arbitrary")), name="ffn",
    )(x, gn, wup, cw, cb, wdn, cprev, *extra_args)


def _to_wkv_sample(a, Bn):
    T = a.shape[0] // Bn
    a = a.reshape(T, Bn, B_HEADS, B_HD).transpose(0, 3, 1, 2).reshape(T, B_HD, Bn * B_HEADS // LANES, LANES)
    return a.transpose(2, 0, 1, 3)


def _o_from_wkv_sample(o, Bn):
    LG, T = o.shape[0], o.shape[1]
    o = o.transpose(1, 2, 0, 3).reshape(T, B_HD, Bn, B_HEADS).transpose(0, 2, 3, 1)
    return o.reshape(T * Bn, B_WIDTH)


def _state_to_wkv_sample(st):
    Bn = st.shape[0]
    st = st.transpose(2, 3, 0, 1).reshape(B_HD, B_HD, Bn * B_HEADS // LANES, LANES)
    return st.transpose(2, 0, 1, 3)


def _state_from_wkv_sample(st, Bn):
    st = st.transpose(1, 2, 0, 3).reshape(B_HD, B_HD, Bn, B_HEADS)
    return st.transpose(2, 3, 0, 1)


def _layer_params(W, e):
    z = jnp.zeros((DECAY_RANK, B_WIDTH), F32)
    head = jnp.arange(B_WIDTH) // B_HD
    same = (head[:, None] == head[None, :])
    return {
        "mu": W["rwkv_mu"][e][None], "w0": W["rwkv_w0"][e][None],
        "w2p": jnp.concatenate([W["rwkv_w2"][e], z], axis=0).astype(BF16),
        "a0": W["rwkv_a0"][e][None],
        "a2p": jnp.concatenate([z, W["rwkv_a2"][e]], axis=0).astype(BF16),
        "g2": W["rwkv_g2"][e].astype(BF16),
        "k_k": W["rwkv_k_k"][e][None], "k_a": W["rwkv_k_a"][e][None],
        "r_k": W["rwkv_r_k"][e].reshape(1, B_WIDTH),
        "seg": same.astype(BF16), "avg": (same.astype(F32) / B_HD).astype(BF16),
    }


def _trunk(x, W, *, prompt, wkv_in, shift_in, pool_in, conv_in, attn_ctx):
    G, T, _ = x.shape
    if prompt:
        Bn, stride, pos0 = G, 1, 0
        tm_in = tm_ffn = tm_pool = min(T, ROW_TILE)
    else:
        Bn, stride, pos0 = attn_ctx["batch"], attn_ctx["batch"], attn_ctx["past_len"]
        tm_in = tm_ffn = tm_pool = T
    N = G * T
    tm_flat = min(N, ROW_TILE)
    k_rows, v_rows, wkv_out, shift_out, pool_out, conv_out = [], [], [], [], [], []
    for l in range(DEPTH):
        gn = W["norm_mix"][l][None]
        if l % 2 == 0:
            e = l // 2
            lam_init = 0.8 - 0.6 * math.exp(-0.3 * l)
            prm = _layer_params(W, e)
            (k, v, qkv, s_new, nkk, d, b, kt, r, vb, g, bonus) = _inproj(
                x, gn, W["w_in_bf"], e, shift_in[e], prm, tm=tm_in, stride=stride)
            lq = W["diff_lambda"][e]
            sub = W["diff_subln"][e][None]
            if prompt:
                oa = _pattn(qkv, lq, sub, lam_init, tq=min(T, ATTN_TILE)).reshape(N, A_WIDTH)
                o, st = _wkv_chunked(nkk, d, b, kt, r, vb)
                o = o.reshape(N, B_WIDTH)
            else:
                nt = T // Bn
                bh = lambda a: a.reshape(nt, Bn, A_HEADS, 2 * A_HD).transpose(1, 2, 0, 3)
                padr = lambda a, rows: jnp.pad(a, ((0, 0), (0, 0), (0, rows - a.shape[2]), (0, 0)))
                q8 = padr(bh(qkv[0, :, :A_WIDTH]), SROWS // 2)
                comp = (jnp.arange(2 * A_HD) // A_HD)
                qh = jnp.concatenate([jnp.where(comp == c, q8, jnp.zeros((), BF16)) for c in range(2)], axis=2)
                oa = _sattn(attn_ctx["page_table"], qh, padr(bh(k[0]), NEW_KEY_ROWS), padr(bh(v[0]), NEW_KEY_ROWS),
                            lq, sub, attn_ctx["cache_k"], attn_ctx["cache_v"], e, nt, lam_init)
                oa = oa[:, :nt].transpose(1, 0, 2).reshape(N, A_WIDTH).astype(BF16)
                streams = [_to_wkv_sample(a[0], Bn) for a in (nkk, d, b, kt, r, vb)]
                o, st = _wkv(*streams, _state_to_wkv_sample(wkv_in[e]), tb=nt)
                o = _o_from_wkv_sample(o, Bn)
                st = _state_from_wkv_sample(st, Bn)
            x = _mixout(x.reshape(N, D_MODEL), oa, o, g.reshape(N, B_WIDTH), bonus.reshape(N, B_WIDTH),
                        W["rwkv_gn_w"][e][None], W["rwkv_gn_b"][e][None], prm["avg"], W["w_out_bf"], e,
                        tm=tm_flat).reshape(G, T, D_MODEL)
            k_rows.append(k)
            v_rows.append(v)
            wkv_out.append(st)
            shift_out.append(s_new)
        else:
            o_ = l // 2
            x, buf = _pool(x, gn, W["pool_w_bf"][o_], W["pool_scale"][o_][None], pool_in[o_],
                           tm=tm_pool, stride=stride, pos0=pos0)
            pool_out.append(buf)
        x, cbuf = _ffn(x, W["norm_ffn"][l][None], W["ffn_up_bf"], W["ffn_conv_w"][l],
                       W["ffn_conv_b"][l][None], W["ffn_down_bf"], l, conv_in[l], tm=tm_ffn, stride=stride,
                       out_gain=W["norm_final"][None] if l == DEPTH - 1 else None)
        conv_out.append(cbuf)
    return (x, jnp.stack(k_rows), jnp.stack(v_rows), jnp.stack(wkv_out), jnp.stack(shift_out),
            jnp.stack(pool_out), jnp.stack(conv_out))


def kernel(x_prompt, x_sample, cache_k, cache_v, state_wkv, state_shift, state_pool, state_ffn_conv,
           page_table, norm_mix, norm_ffn, norm_final, w_in, w_out, diff_lambda, diff_subln,
           rwkv_mu, rwkv_w0, rwkv_w2, rwkv_a0, rwkv_a2, rwkv_g2, rwkv_k_k, rwkv_k_a, rwkv_r_k,
           rwkv_gn_w, rwkv_gn_b, pool_w, pool_scale, ffn_up, ffn_conv_w, ffn_conv_b, ffn_down):
    W = dict(norm_mix=norm_mix, norm_ffn=norm_ffn, norm_final=norm_final,
             diff_lambda=diff_lambda, diff_subln=diff_subln, rwkv_mu=rwkv_mu, rwkv_w0=rwkv_w0,
             rwkv_w2=rwkv_w2, rwkv_a0=rwkv_a0, rwkv_a2=rwkv_a2, rwkv_g2=rwkv_g2, rwkv_k_k=rwkv_k_k,
             rwkv_k_a=rwkv_k_a, rwkv_r_k=rwkv_r_k, rwkv_gn_w=rwkv_gn_w, rwkv_gn_b=rwkv_gn_b,
             pool_scale=pool_scale, ffn_conv_w=ffn_conv_w, ffn_conv_b=ffn_conv_b,
             w_in_bf=w_in.astype(BF16), w_out_bf=w_out.astype(BF16), pool_w_bf=pool_w.astype(BF16),
             ffn_up_bf=ffn_up.astype(BF16), ffn_down_bf=ffn_down.astype(BF16))
    n_even, n_odd = state_wkv.shape[0], state_pool.shape[0]
    Bp, Sp, _ = x_prompt.shape
    Bs, Ts, _ = x_sample.shape
    n_pages = page_table.shape[1]
    past_len = n_pages * cache_k.shape[2]

    y_p, k_p, v_p, wkv_p, shift_p, pool_p, conv_p = _trunk(
        x_prompt, W, prompt=True, wkv_in=None,
        shift_in=jnp.zeros((n_even, Bp, 1, B_COLS), F32),
        pool_in=jnp.zeros((n_odd, Bp, POOL_BUF, D_MODEL), F32),
        conv_in=jnp.zeros((DEPTH, Bp, CONV_W - 1, D_FF), F32), attn_ctx=None)
    k_p = k_p.reshape(n_even, Bp, Sp, A_HEADS, 2 * A_HD)
    v_p = v_p.reshape(n_even, Bp, Sp, A_HEADS, 2 * A_HD)
    shift_p = shift_p.reshape(n_even, Bp, B_COLS)

    tmaj = lambda a: jnp.swapaxes(a, -3, -2)
    flat = lambda a: a.reshape(a.shape[:-3] + (1, a.shape[-3] * a.shape[-2], a.shape[-1]))
    ctx = dict(batch=Bs, past_len=past_len, page_table=page_table,
               cache_k=cache_k.reshape(cache_k.shape[:2] + (PAGE_SIZE * A_HEADS, 2 * A_HD)),
               cache_v=cache_v.reshape(cache_v.shape[:2] + (PAGE_SIZE * A_HEADS, 2 * A_HD)))
    y_s, k_s, v_s, wkv_s, shift_s, pool_s, conv_s = _trunk(
        flat(tmaj(x_sample)), W, prompt=False, wkv_in=state_wkv,
        shift_in=state_shift[:, None], pool_in=flat(tmaj(state_pool)),
        conv_in=flat(tmaj(state_ffn_conv)), attn_ctx=ctx)
    unflat = lambda a, j: tmaj(a.reshape(a.shape[:-3] + (j, Bs, a.shape[-1])))
    y_s = unflat(y_s, Ts)
    k_s = k_s.reshape(n_even, Ts, Bs, A_HEADS, 2 * A_HD).transpose(0, 2, 1, 3, 4)
    v_s = v_s.reshape(n_even, Ts, Bs, A_HEADS, 2 * A_HD).transpose(0, 2, 1, 3, 4)
    shift_s = shift_s.reshape(n_even, Bs, B_COLS)
    pool_s = unflat(pool_s, POOL_BUF)
    conv_s = unflat(conv_s, CONV_W - 1)
    return (y_p, y_s, k_p, v_p, wkv_p, shift_p, pool_p, conv_p,
            k_s, v_s, wkv_s, shift_s, pool_s, conv_s)
```

```python
import functools
import math

import jax
import jax.numpy as jnp
from jax import lax
from jax.experimental import pallas as pl
from jax.experimental.pallas import tpu as pltpu

F32 = jnp.float32
BF16 = jnp.bfloat16

D_MODEL = 1024
DEPTH = 4
PAGE_SIZE = 128
A_HD = 64
A_HEADS = 4
A_WIDTH = 512
A_COLS = 3 * A_WIDTH
A_SCALE = A_HD ** -0.5
Q_SCALE = A_SCALE * math.log2(math.e)
B_HD = 64
B_WIDTH = 512
B_HEADS = 8
DECAY_RANK = 64
ICLR_RANK = 64
GATE_RANK = 128
B_COLS = 3 * B_WIDTH + DECAY_RANK + ICLR_RANK + GATE_RANK
IN_COLS = A_COLS + B_COLS
POOL_WINDOWS = (2, 4, 8, 16)
POOL_GW = D_MODEL // len(POOL_WINDOWS)
POOL_BUF = max(POOL_WINDOWS) - 1
D_FF = 2816
CONV_W = 3
NORM_EPS = 1e-6
SUBLN_EPS = 1e-5
GN_EPS = 64e-5

LANES = 128
SUBLANES = 8
VMEM_LIMIT = 56 * 1024 * 1024
ROW_TILE = 512
ATTN_TILE = 1024
FF_CHUNK = 1408
PAGES_PER_STEP = 32
NEW_KEY_ROWS = 16
SROWS = 16
HEAD_LOG2 = 6


def _div_nonneg(x, n):
    if n == 1:
        return x
    if n & (n - 1) == 0:
        return lax.shift_right_logical(x, n.bit_length() - 1)
    return x // n


def _cparams(sem):
    return pltpu.CompilerParams(dimension_semantics=sem, vmem_limit_bytes=VMEM_LIMIT)


def _const_spec(shape):
    nd = len(shape)
    return pl.BlockSpec(shape, lambda *_: (0,) * nd, pipeline_mode=pl.Buffered(1))


def _layer_spec(shape, li):
    nd = len(shape)
    return pl.BlockSpec((None,) + tuple(shape), lambda *_: (li,) + (0,) * nd, pipeline_mode=pl.Buffered(1))


def _rms(x, g, eps):
    return x * lax.rsqrt(jnp.mean(x * x, axis=-1, keepdims=True) + eps) * g


def _bf16_dot(x, m):
    return jnp.dot(x.astype(BF16), m, preferred_element_type=F32)


def _lane_tile(x, n):
    return jnp.concatenate([x] * n, axis=1)


def _softplus(z):
    return jnp.maximum(z, 0.0) + jnp.log1p(jnp.exp(-jnp.abs(z)))


def _inproj_kernel(x_ref, gn_ref, win_ref, sprev_ref, mu_ref, w0_ref, w2p_ref, a0_ref, a2p_ref,
                   g2_ref, kk_ref, ka_ref, rk_ref, seg_ref,
                   k_ref, v_ref, qkv_ref, sout_ref, nkk_ref, d_ref, b_ref, kt_ref, r_ref,
                   vb_ref, g_ref, bonus_ref, ext_ref, *, tm, stride, halo):
    s = pl.program_id(1)
    ns = pl.num_programs(1)
    h = _rms(x_ref[...], gn_ref[...], NORM_EPS).astype(BF16)
    p = jnp.dot(h, win_ref[...], preferred_element_type=F32)
    for hh in range(A_HEADS):
        c0 = A_WIDTH + hh * 2 * A_HD
        k_ref[pl.ds(hh, tm, stride=A_HEADS), :] = p[:, c0:c0 + 2 * A_HD]
        v_ref[pl.ds(hh, tm, stride=A_HEADS), :] = p[:, A_WIDTH + c0:A_WIDTH + c0 + 2 * A_HD]
    qkv_ref[:, :A_WIDTH] = (p[:, :A_WIDTH] * Q_SCALE).astype(BF16)
    qkv_ref[:, A_WIDTH:] = p[:, A_WIDTH:A_COLS].astype(BF16)
    pb = p[:, A_COLS:]

    @pl.when(s == 0)
    def _():
        ext_ref[halo - stride:halo, :] = sprev_ref[...]

    ext_ref[halo:halo + tm, :] = pb
    prev = ext_ref[halo - stride:halo - stride + tm, :]
    ext_ref[halo - stride:halo, :] = ext_ref[halo + tm - stride:halo + tm, :]

    @pl.when(s == ns - 1)
    def _():
        sout_ref[...] = ext_ref[halo + tm - stride:halo + tm, :]

    xm = pb + (prev - pb) * mu_ref[...]
    r = xm[:, :B_WIDTH]
    k = xm[:, B_WIDTH:2 * B_WIDTH]
    v = xm[:, 2 * B_WIDTH:3 * B_WIDTH]
    wa = xm[:, 3 * B_WIDTH:3 * B_WIDTH + DECAY_RANK + ICLR_RANK]
    gd = xm[:, 3 * B_WIDTH + DECAY_RANK + ICLR_RANK:]
    wlin = w0_ref[...] + jnp.dot(jnp.tanh(wa).astype(BF16), w2p_ref[...], preferred_element_type=F32)
    w = -_softplus(-wlin) - 0.5
    log_decay = -jnp.exp(w)
    a = jax.nn.sigmoid(a0_ref[...] + jnp.dot(wa.astype(BF16), a2p_ref[...], preferred_element_type=F32))
    g = jnp.dot(jax.nn.sigmoid(gd).astype(BF16), g2_ref[...], preferred_element_type=F32)
    kk = k * kk_ref[...]
    ss = _bf16_dot(kk * kk, seg_ref[...])
    kk = kk / jnp.maximum(jnp.sqrt(ss), 1e-12)
    kt = k * (1.0 + (a - 1.0) * ka_ref[...])
    bonus = _bf16_dot(r * kt * rk_ref[...], seg_ref[...]) * v
    nkk_ref[...] = -kk
    d_ref[...] = log_decay
    b_ref[...] = kk * a
    kt_ref[...] = kt
    r_ref[...] = r
    vb_ref[...] = v
    g_ref[...] = g
    bonus_ref[...] = bonus


def _inproj(x, gn, win, li, sprev, prm, *, tm, stride):
    G, T, _ = x.shape
    halo = max(SUBLANES, stride)
    ns = T // tm
    row = lambda w: pl.BlockSpec((None, tm, w), lambda g, s: (g, s, 0))
    stream = jax.ShapeDtypeStruct((G, T, B_WIDTH), F32)
    in_specs = [row(D_MODEL), _const_spec((1, D_MODEL)), _layer_spec((D_MODEL, IN_COLS), li),
                pl.BlockSpec((None, stride, B_COLS), lambda g, s: (g, 0, 0)),
                _const_spec((1, B_COLS)), _const_spec((1, B_WIDTH)), _const_spec((LANES, B_WIDTH)),
                _const_spec((1, B_WIDTH)), _const_spec((LANES, B_WIDTH)), _const_spec((GATE_RANK, B_WIDTH)),
                _const_spec((1, B_WIDTH)), _const_spec((1, B_WIDTH)), _const_spec((1, B_WIDTH)),
                _const_spec((B_WIDTH, B_WIDTH))]
    kv_shape = jax.ShapeDtypeStruct((G, T * A_HEADS, 2 * A_HD), F32)
    kv_spec = pl.BlockSpec((None, tm * A_HEADS, 2 * A_HD), lambda g, s: (g, s, 0))
    out_shape = [kv_shape, kv_shape, jax.ShapeDtypeStruct((G, T, A_COLS), BF16),
                 jax.ShapeDtypeStruct((G, stride, B_COLS), F32)] + [stream] * 8
    out_specs = [kv_spec, kv_spec, row(A_COLS),
                 pl.BlockSpec((None, stride, B_COLS), lambda g, s: (g, 0, 0))] + [row(B_WIDTH)] * 8
    return pl.pallas_call(
        functools.partial(_inproj_kernel, tm=tm, stride=stride, halo=halo),
        grid=(G, ns), in_specs=in_specs, out_specs=out_specs, out_shape=out_shape,
        scratch_shapes=[pltpu.VMEM((halo + tm, B_COLS), F32)],
        compiler_params=_cparams(("arbitrary", "arbitrary")), name="inproj",
    )(x, gn, win, sprev, prm["mu"], prm["w0"], prm["w2p"], prm["a0"], prm["a2p"], prm["g2"],
      prm["k_k"], prm["k_a"], prm["r_k"], prm["seg"])


def _lambda_full(lq, lam_init):
    s01 = jnp.sum(lq[0:1, :] * lq[1:2, :], axis=-1, keepdims=True)
    s23 = jnp.sum(lq[2:3, :] * lq[3:4, :], axis=-1, keepdims=True)
    return jnp.exp(s01) - jnp.exp(s23) + lam_init


def _pattn_kernel(qi_ref, kj_ref, q_ref, k_ref, v_ref, lq_ref, sub_ref, o_ref,
                  m0_ref, l0_ref, a0_ref, m1_ref, l1_ref, a1_ref, *, tq, lam_init):
    qi = qi_ref[pl.program_id(2)]
    kj = kj_ref[pl.program_id(2)]

    @pl.when(kj == 0)
    def _():
        for m_ref, l_ref, a_ref in ((m0_ref, l0_ref, a0_ref), (m1_ref, l1_ref, a1_ref)):
            m_ref[...] = jnp.full(m_ref.shape, -jnp.inf, F32)
            l_ref[...] = jnp.zeros(l_ref.shape, F32)
            a_ref[...] = jnp.zeros(a_ref.shape, F32)

    def accumulate(diagonal):
        q = q_ref[...]
        k = k_ref[...]
        v = v_ref[...]
        lane = lax.broadcasted_iota(jnp.int32, q.shape, 1)
        zero = jnp.zeros_like(q)
        if diagonal:
            keep = (lax.broadcasted_iota(jnp.int32, (tq, tq), 1)
                    <= lax.broadcasted_iota(jnp.int32, (tq, tq), 0))
        refs = ((m0_ref, l0_ref, a0_ref), (m1_ref, l1_ref, a1_ref))
        comps = range(2)
        qc = [jnp.where((lane >= c * A_HD) & (lane < (c + 1) * A_HD), q, zero) for c in comps]
        s = [lax.dot_general(qc[c], k, (((1,), (1,)), ((), ())), preferred_element_type=F32) for c in comps]
        if diagonal:
            s = [jnp.where(keep, s[c], -jnp.inf) for c in comps]
        m_old = [refs[c][0][...] for c in comps]
        m_new = [jnp.maximum(m_old[c], jnp.max(s[c], axis=-1, keepdims=True)) for c in comps]
        alpha = [jnp.exp2(m_old[c] - m_new[c]) for c in comps]
        p = [jnp.exp2(s[c] - _lane_tile(m_new[c], tq // LANES)) for c in comps]
        for c in comps:
            m_ref, l_ref, a_ref = refs[c]
            l_ref[...] = alpha[c] * l_ref[...] + jnp.sum(p[c], axis=-1, keepdims=True)
            a_ref[...] = alpha[c] * a_ref[...] + jnp.dot(p[c].astype(BF16), v, preferred_element_type=F32)
            m_ref[...] = m_new[c]

    @pl.when(kj < qi)
    def _():
        accumulate(False)

    @pl.when(kj == qi)
    def _():
        accumulate(True)
        lam = _lambda_full(lq_ref[...], lam_init)
        o = a0_ref[...] / l0_ref[...] - lam * (a1_ref[...] / l1_ref[...])
        o = _rms(o, sub_ref[...], SUBLN_EPS) * (1.0 - lam_init)
        o_ref[...] = o.astype(o_ref.dtype)


def _pattn(qkv, lq, sub, lam_init, *, tq):
    B, S, _ = qkv.shape
    nq = S // tq
    nh = A_HEADS
    pairs = [(i, j) for i in range(nq) for j in range(i + 1)]
    qi_tab = jnp.asarray([p[0] for p in pairs], jnp.int32)
    kj_tab = jnp.asarray([p[1] for p in pairs], jnp.int32)
    q_spec = pl.BlockSpec((None, tq, LANES), lambda b, h, n, qt, kt: (b, qt[n], h))
    k_spec = pl.BlockSpec((None, tq, LANES), lambda b, h, n, qt, kt: (b, kt[n], nh + h))
    v_spec = pl.BlockSpec((None, tq, LANES), lambda b, h, n, qt, kt: (b, kt[n], 2 * nh + h))
    grid_spec = pltpu.PrefetchScalarGridSpec(
        num_scalar_prefetch=2, grid=(B, nh, len(pairs)),
        in_specs=[q_spec, k_spec, v_spec,
                  pl.BlockSpec((4, A_HD), lambda b, h, n, qt, kt: (0, 0)),
                  pl.BlockSpec((1, LANES), lambda b, h, n, qt, kt: (0, 0))],
        out_specs=pl.BlockSpec((None, tq, LANES), lambda b, h, n, qt, kt: (b, qt[n], h)),
        scratch_shapes=[pltpu.VMEM((tq, LANES), F32)] * 6)
    return pl.pallas_call(
        functools.partial(_pattn_kernel, tq=tq, lam_init=lam_init),
        grid_spec=grid_spec, out_shape=jax.ShapeDtypeStruct((B, S, A_WIDTH), BF16),
        compiler_params=_cparams(("arbitrary",) * 3), name="pattn",
    )(qi_tab, kj_tab, qkv, qkv, qkv, lq, sub)


def _sattn_kernel(pt_ref, q_ref, kn_ref, vn_ref, lq_ref, sub_ref, *rest, npg, nt, lam_init):
    k_refs = rest[:npg]
    v_refs = rest[npg:2 * npg]
    o_ref = rest[2 * npg]
    m_ref, l_ref, acc_ref = rest[2 * npg + 1:]
    j = pl.program_id(1)
    nj = pl.num_programs(1)

    @pl.when(j == 0)
    def _():
        m_ref[...] = jnp.full(m_ref.shape, -jnp.inf, F32)
        l_ref[...] = jnp.zeros(l_ref.shape, F32)
        acc_ref[...] = jnp.zeros(acc_ref.shape, F32)

    nt_dims = (((1,), (1,)), ((), ()))
    qs = [q_ref[h] for h in range(A_HEADS)]

    def head_rows(pg_ref, h):
        return pg_ref[pl.ds(h, PAGE_SIZE, stride=A_HEADS), :].astype(BF16)

    def update(s, pv):
        m_old = m_ref[...]
        m_new = jnp.maximum(m_old, jnp.max(s, axis=-1, keepdims=True))
        alpha = jnp.exp2(m_old - m_new)
        p = jnp.exp2(s - (_lane_tile(m_new, s.shape[1] // LANES) if s.shape[1] > LANES
                          else m_new[:, :s.shape[1]]))
        l_ref[...] = alpha * l_ref[...] + jnp.sum(p, axis=-1, keepdims=True)
        acc_ref[...] = alpha * acc_ref[...] + pv(p.astype(BF16))
        m_ref[...] = m_new

    s_heads = [jnp.concatenate(
        [lax.dot_general(qs[h], head_rows(kr, h), nt_dims, preferred_element_type=F32) for kr in k_refs],
        axis=1) for h in range(A_HEADS)]

    def pv_pages(p):
        outs = []
        for h in range(A_HEADS):
            ph = p[h * SROWS:(h + 1) * SROWS, :]
            tot = None
            for i, vr in enumerate(v_refs):
                part = jnp.dot(ph[:, i * PAGE_SIZE:(i + 1) * PAGE_SIZE], head_rows(vr, h),
                               preferred_element_type=F32)
                tot = part if tot is None else tot + part
            outs.append(tot)
        return jnp.concatenate(outs, axis=0)

    update(jnp.concatenate(s_heads, axis=0), pv_pages)

    @pl.when(j == nj - 1)
    def _():
        s_new = jnp.concatenate(
            [lax.dot_general(qs[h], kn_ref[h].astype(BF16), nt_dims, preferred_element_type=F32)
             for h in range(A_HEADS)], axis=0)
        key = lax.broadcasted_iota(jnp.int32, s_new.shape, 1)
        tok = lax.broadcasted_iota(jnp.int32, s_new.shape, 0) & (SROWS // 2 - 1)
        s_new = jnp.where((key <= tok) & (key < nt), s_new, -jnp.inf)

        def pv_new(p):
            return jnp.concatenate(
                [jnp.dot(p[h * SROWS:(h + 1) * SROWS, :], vn_ref[h].astype(BF16), preferred_element_type=F32)
                 for h in range(A_HEADS)], axis=0)

        update(s_new, pv_new)
        lam = _lambda_full(lq_ref[...], lam_init)
        half = SROWS // 2
        outs = []
        for h in range(A_HEADS):
            r0 = h * SROWS
            o = (acc_ref[r0:r0 + half, :] / l_ref[r0:r0 + half, :]
                 - lam * (acc_ref[r0 + half:r0 + SROWS, :] / l_ref[r0 + half:r0 + SROWS, :]))
            outs.append(_rms(o, sub_ref[...], SUBLN_EPS) * (1.0 - lam_init))
        o_ref[...] = jnp.concatenate(outs, axis=-1)


def _sattn(page_table, qh, k_new, v_new, lq, sub, kc, vc, e, nt, lam_init):
    Bn = qh.shape[0]
    n_pages = page_table.shape[1]
    npg = PAGES_PER_STEP
    nj = n_pages // npg
    page_rows = PAGE_SIZE * A_HEADS
    per_b = lambda r: pl.BlockSpec((None, A_HEADS, r, LANES), lambda b, j, pt: (b, 0, 0, 0))

    def page_spec(i):
        return pl.BlockSpec((None, None, page_rows, LANES), lambda b, j, pt: (e, pt[b, j * npg + i], 0, 0))

    in_specs = ([per_b(SROWS), per_b(NEW_KEY_ROWS), per_b(NEW_KEY_ROWS),
                 pl.BlockSpec((4, A_HD), lambda b, j, pt: (0, 0)),
                 pl.BlockSpec((1, LANES), lambda b, j, pt: (0, 0))]
                + [page_spec(i) for i in range(npg)] * 2)
    nr = A_HEADS * SROWS
    grid_spec = pltpu.PrefetchScalarGridSpec(
        num_scalar_prefetch=1, grid=(Bn, nj), in_specs=in_specs,
        out_specs=pl.BlockSpec((None, SROWS // 2, A_WIDTH), lambda b, j, pt: (b, 0, 0)),
        scratch_shapes=[pltpu.VMEM((nr, LANES), F32)] * 3)
    return pl.pallas_call(
        functools.partial(_sattn_kernel, npg=npg, nt=nt, lam_init=lam_init),
        grid_spec=grid_spec, out_shape=jax.ShapeDtypeStruct((Bn, SROWS // 2, A_WIDTH), F32),
        compiler_params=_cparams(("arbitrary", "arbitrary")), name="sattn",
    )(page_table, qh, k_new, v_new, lq, sub, *([kc] * npg), *([vc] * npg))


def _wkv_kernel(nkk_ref, d_ref, b_ref, kt_ref, r_ref, v_ref, s0_ref, o_ref, st_ref, s_scr, *, tb, nv):
    t0 = pl.program_id(1)

    @pl.when(t0 == 0)
    def _():
        s_scr[...] = s0_ref[...]

    def step(t, carry):
        nkk = nkk_ref[t]
        d = jnp.exp(d_ref[t])
        b = b_ref[t]
        kt = kt_ref[t]
        r = r_ref[t]
        dr = d * r
        br = jnp.sum(b * r, axis=0, keepdims=True)
        kr = jnp.sum(kt * r, axis=0, keepdims=True)

        def row(vr, c):
            S = s_scr[vr]
            sa = jnp.sum(S * nkk, axis=0, keepdims=True)
            qo = jnp.sum(S * dr, axis=0, keepdims=True)
            vv = v_ref[t, pl.ds(vr, 1), :]
            s_scr[vr] = S * d + sa * b + vv * kt
            o_ref[t, pl.ds(vr, 1), :] = qo + sa * br + vv * kr
            return c

        return lax.fori_loop(0, nv, row, carry, unroll=2)

    lax.fori_loop(0, tb, step, 0)
    st_ref[...] = s_scr[...]


def _wkv(nkk, d, b, kt, r, v, s0, *, tb):
    LG, T, _, _ = nkk.shape
    nv = v.shape[2]
    kspec = pl.BlockSpec((None, tb, B_HD, LANES), lambda g, t: (g, t, 0, 0))
    vspec = pl.BlockSpec((None, tb, nv, LANES), lambda g, t: (g, t, 0, 0))
    sspec = pl.BlockSpec((None, nv, B_HD, LANES), lambda g, t: (g, 0, 0, 0))
    return pl.pallas_call(
        functools.partial(_wkv_kernel, tb=tb, nv=nv),
        grid=(LG, T // tb), in_specs=[kspec] * 5 + [vspec, sspec],
        out_specs=[vspec, sspec],
        out_shape=[jax.ShapeDtypeStruct((LG, T, nv, LANES), F32),
                   jax.ShapeDtypeStruct((LG, nv, B_HD, LANES), F32)],
        scratch_shapes=[pltpu.VMEM((nv, B_HD, LANES), F32)],
        compiler_params=_cparams(("arbitrary", "arbitrary")), name="wkv",
    )(nkk, d, b, kt, r, v, s0)


WKV_CHUNK = 64
WKV_GROUP = 4


def _split3_dot(m, x):
    x1 = x.astype(BF16)
    r1 = x - x1.astype(F32)
    x2 = r1.astype(BF16)
    x3 = (r1 - x2.astype(F32)).astype(BF16)
    return (jnp.dot(m, x1, preferred_element_type=F32) + jnp.dot(m, x2, preferred_element_type=F32)
            + jnp.dot(m, x3, preferred_element_type=F32))


def _wkv_chunk_math(a, ld, b, kt, r, v, s):
    L, HW = WKV_CHUNK, WKV_GROUP * B_HD
    nt_dims = (((1,), (1,)), ((), ()))
    tn_dims = (((0,), (0,)), ((), ()))
    row = lax.broadcasted_iota(jnp.int32, (L, HW), 0)
    col = lax.broadcasted_iota(jnp.int32, (L, HW), 1) & (B_HD - 1)
    strict = col < row
    incl = col <= row
    blk = lax.shift_right_logical
    bd_mask = (lax.shift_right_logical(lax.broadcasted_iota(jnp.int32, (HW, HW), 0), HEAD_LOG2)
               == lax.shift_right_logical(lax.broadcasted_iota(jnp.int32, (HW, HW), 1), HEAD_LOG2))

    def bd(y):
        y = y.astype(BF16)
        return jnp.where(bd_mask, jnp.concatenate([y] * WKV_GROUP, axis=0), jnp.zeros((), BF16))

    def hprod(x, y):
        return jnp.dot(x.astype(BF16), bd(y), preferred_element_type=F32)

    def each(f, *lists):
        return [f(*xs) for xs in zip(*lists)]

    def dotg(dims):
        return lambda x, y: lax.dot_general(x, y, dims, preferred_element_type=F32)

    tri = (lax.broadcasted_iota(jnp.int32, (L, L), 1) <= lax.broadcasted_iota(jnp.int32, (L, L), 0)).astype(BF16)
    logp = each(lambda x: _split3_dot(tri, x), ld)
    ep = each(jnp.exp, logp)
    p_last = each(lambda x: x[L - 1:L, :], ep)
    en = each(lambda x: jnp.exp(-x), logp)
    at = each(lambda x, lp, l_: x * jnp.exp(lp - l_), a, logp, ld)
    rt = each(jnp.multiply, r, ep)
    bt = each(jnp.multiply, b, en)
    ktt = each(jnp.multiply, kt, en)
    bh = each(lambda x, pp: (x * pp).astype(BF16), bt, p_last)
    kh = each(lambda x, pp: (x * pp).astype(BF16), ktt, p_last)

    lhs = each(lambda x, y: jnp.concatenate([x, y], axis=0).astype(BF16), at, rt)
    xb = each(lambda x, y: dotg(nt_dims)(x, bd(y)), lhs, bt)
    xk = each(lambda x, y: dotg(nt_dims)(x, bd(y)), lhs, ktt)
    n = each(lambda x: jnp.where(strict, x[:L], 0.0), xb)
    arb = each(lambda x: jnp.where(incl, x[L:], 0.0), xb)
    aak = each(lambda x: jnp.where(strict, x[:L], 0.0), xk)
    ark = each(lambda x: jnp.where(incl, x[L:], 0.0), xk)

    eye = jnp.where(col == row, 1.0, 0.0)
    tm = each(lambda x: eye + jnp.where(blk(row, 1) == blk(col, 1), x, 0.0), n)
    for lg in range(1, HEAD_LOG2):
        sel = (blk(row, lg + 1) == blk(col, lg + 1)) & (blk(row, lg) != blk(col, lg))
        w = each(lambda t_, x: hprod(t_, jnp.where(sel, x, 0.0)), tm, n)
        tm = each(lambda t_, w_: t_ + hprod(w_, t_), tm, w)

    rloc = each(hprod, aak, v)
    uloc = each(hprod, tm, rloc)
    ah = each(hprod, tm, at)
    ro = each(lambda x, y, z: (x + hprod(y, z)).astype(BF16), rt, arb, ah)
    oc = each(lambda x, u, y, vv: hprod(x, u) + hprod(y, vv), arb, uloc, ark, v)
    mpp = each(lambda x, y: jnp.where(bd_mask, dotg(tn_dims)(x.astype(BF16), y), 0.0).astype(BF16), ah, bh)
    sc = each(lambda u, y, vv, z: jnp.where(
        bd_mask, dotg(tn_dims)(u.astype(BF16), y) + dotg(tn_dims)(vv.astype(BF16), z), 0.0), uloc, bh, v, kh)

    s_hi = each(lambda x: x.astype(BF16), s)
    s_lo = each(lambda x, h: (x - h.astype(F32)).astype(BF16), s, s_hi)
    o = each(lambda q, h, c_: dotg(nt_dims)(q, h) + c_, ro, s_hi, oc)
    s_new = each(lambda x, pp, h, l_, m_, c_: (x * pp + jnp.dot(h, m_, preferred_element_type=F32)
                                               + jnp.dot(l_, m_, preferred_element_type=F32) + c_),
                 s, p_last, s_hi, s_lo, mpp, sc)
    return o, s_new


def _wkv_chunk_kernel(a_ref, ld_ref, b_ref, kt_ref, r_ref, v_ref, o_ref, st_ref, s_scr, *, nb):
    c = pl.program_id(1)
    nc = pl.num_programs(1)
    HW = WKV_GROUP * B_HD

    @pl.when(c == 0)
    def _():
        s_scr[...] = jnp.zeros(s_scr.shape, F32)

    groups = [(bb, g) for bb in range(nb) for g in range(B_WIDTH // HW)]
    sls = [(bb, slice(None), slice(g * HW, (g + 1) * HW)) for bb, g in groups]
    o, s_new = _wkv_chunk_math(*[[ref[sl] for sl in sls] for ref in (a_ref, ld_ref, b_ref, kt_ref, r_ref, v_ref)],
                               [s_scr[bb, g] for bb, g in groups])
    for sl, (bb, g), o_i, s_i in zip(sls, groups, o, s_new):
        o_ref[sl] = o_i
        s_scr[bb, g] = s_i

    @pl.when(c == nc - 1)
    def _():
        st_ref[...] = s_scr[...]


def _wkv_chunked(a, ld, b, kt, r, v, *, nb=4):
    Bn, T, _ = a.shape
    L, HW = WKV_CHUNK, WKV_GROUP * B_HD
    ng = B_WIDTH // HW
    nb = min(nb, Bn)
    assert Bn % nb == 0 and T % L == 0
    spec = pl.BlockSpec((nb, L, B_WIDTH), lambda bb, c: (bb, c, 0))
    st_spec = pl.BlockSpec((nb, ng, HW, HW), lambda bb, c: (bb, 0, 0, 0))
    o, st = pl.pallas_call(
        functools.partial(_wkv_chunk_kernel, nb=nb),
        grid=(Bn // nb, T // L), in_specs=[spec] * 6, out_specs=[spec, st_spec],
        out_shape=[jax.ShapeDtypeStruct((Bn, T, B_WIDTH), F32), jax.ShapeDtypeStruct((Bn, ng, HW, HW), F32)],
        scratch_shapes=[pltpu.VMEM((nb, ng, HW, HW), F32)],
        compiler_params=_cparams(("arbitrary",) * 2), name="wkv_chunked",
    )(a, ld, b, kt, r, v)
    st = st.reshape(Bn, ng, WKV_GROUP, B_HD, WKV_GROUP, B_HD)
    st = jnp.stack([st[:, :, i, :, i, :] for i in range(WKV_GROUP)], axis=2)
    return o, st.reshape(Bn, B_HEADS, B_HD, B_HD)


def _mixout_kernel(x_ref, oa_ref, o_ref, g_ref, bonus_ref, gnw_ref, gnb_ref, avg_ref, wout_ref, xo_ref):
    o = o_ref[...]
    mu = _bf16_dot(o, avg_ref[...])
    dlt = o - mu
    var = _bf16_dot(dlt * dlt, avg_ref[...])
    on = dlt * lax.rsqrt(var + GN_EPS) * gnw_ref[...] + gnb_ref[...]
    ob = ((on + bonus_ref[...]) * g_ref[...]).astype(BF16)
    y = (jnp.dot(oa_ref[...], wout_ref[0:A_WIDTH, :], preferred_element_type=F32)
         + jnp.dot(ob, wout_ref[A_WIDTH:, :], preferred_element_type=F32))
    xo_ref[...] = x_ref[...] + y


def _mixout(x, oa, o, g, bonus, gnw, gnb, avg, wout, li, *, tm):
    N = x.shape[0]
    row = lambda w: pl.BlockSpec((tm, w), lambda i: (i, 0))
    return pl.pallas_call(
        _mixout_kernel, grid=(N // tm,),
        in_specs=[row(D_MODEL), row(A_WIDTH), row(B_WIDTH), row(B_WIDTH), row(B_WIDTH),
                  _const_spec((1, B_WIDTH)), _const_spec((1, B_WIDTH)),
                  _const_spec((B_WIDTH, B_WIDTH)), _layer_spec((D_MODEL, D_MODEL), li)],
        out_specs=row(D_MODEL), out_shape=jax.ShapeDtypeStruct((N, D_MODEL), F32),
        compiler_params=_cparams(("arbitrary",)), name="mixout",
    )(x, oa, o, g, bonus, gnw, gnb, avg, wout)


def _pool_kernel(x_ref, gn_ref, pw_ref, ps_ref, pprev_ref, xo_ref, pout_ref, ext_ref,
                 *, tm, stride, halo, pos0):
    s = pl.program_id(1)
    ns = pl.num_programs(1)
    keep = POOL_BUF * stride
    x = x_ref[...]
    h = _rms(x, gn_ref[...], NORM_EPS)

    @pl.when(s == 0)
    def _():
        ext_ref[halo - keep:halo, :] = pprev_ref[...]

    ext_ref[halo:halo + tm, :] = h
    t = _div_nonneg(s * tm + lax.broadcasted_iota(jnp.int32, (tm, 1), 0), stride)
    ys = []
    for gi, win in enumerate(POOL_WINDOWS):
        c0, c1 = gi * POOL_GW, (gi + 1) * POOL_GW
        cur = ext_ref[halo:halo + tm, c0:c1]
        wsum = cur
        for jj in range(1, win):
            wsum = wsum + ext_ref[halo - jj * stride:halo - jj * stride + tm, c0:c1]
        cnt = jnp.minimum(pos0 + t + 1, win).astype(F32)
        m = wsum / cnt - cur
        ys.append(jnp.dot(m.astype(BF16), pw_ref[gi], preferred_element_type=F32))
    xo_ref[...] = x + jnp.concatenate(ys, axis=-1) * ps_ref[...]

    @pl.when(s == ns - 1)
    def _():
        pout_ref[...] = ext_ref[halo + tm - keep:halo + tm, :]

    if tm >= keep:
        ext_ref[halo - keep:halo, :] = ext_ref[halo + tm - keep:halo + tm, :]


def _pool(x, gn, pw, ps, pprev, *, tm, stride, pos0):
    G, T, _ = x.shape
    halo = 2 * SUBLANES * stride
    keep = POOL_BUF * stride
    assert T // tm == 1 or tm >= keep
    row = pl.BlockSpec((None, tm, D_MODEL), lambda g, s: (g, s, 0))
    st = pl.BlockSpec((None, keep, D_MODEL), lambda g, s: (g, 0, 0))
    return pl.pallas_call(
        functools.partial(_pool_kernel, tm=tm, stride=stride, halo=halo, pos0=pos0),
        grid=(G, T // tm),
        in_specs=[row, _const_spec((1, D_MODEL)), _const_spec((len(POOL_WINDOWS), POOL_GW, POOL_GW)),
                  _const_spec((1, D_MODEL)), st],
        out_specs=[row, st],
        out_shape=[jax.ShapeDtypeStruct((G, T, D_MODEL), F32), jax.ShapeDtypeStruct((G, keep, D_MODEL), F32)],
        scratch_shapes=[pltpu.VMEM((halo + tm, D_MODEL), F32)],
        compiler_params=_cparams(("arbitrary", "arbitrary")), name="pool",
    )(x, gn, pw, ps, pprev)


def _ffn_kernel(x_ref, gn_ref, wup_ref, cw_ref, cb_ref, wdn_ref, cprev_ref, *rest, tm, stride, halo, out_norm):
    gout_ref = rest[0] if out_norm else None
    xo_ref, cout_ref, ext_ref = rest[-3:]
    s = pl.program_id(1)
    ns = pl.num_programs(1)
    keep = (CONV_W - 1) * stride
    x = x_ref[...]
    h = _rms(x, gn_ref[...], NORM_EPS).astype(BF16)

    @pl.when(s == 0)
    def _():
        ext_ref[halo - keep:halo, :] = cprev_ref[...]

    def up(c):
        c0, c1 = c * FF_CHUNK, (c + 1) * FF_CHUNK
        gate = jnp.dot(h, wup_ref[:, c0:c1], preferred_element_type=F32)
        val = jnp.dot(h, wup_ref[:, D_FF + c0:D_FF + c1], preferred_element_type=F32)
        ext_ref[halo:halo + tm, c0:c1] = gate
        return gate, val

    def down(c, gate, val):
        c0, c1 = c * FF_CHUNK, (c + 1) * FF_CHUNK
        cc = cb_ref[:, c0:c1]
        for jj in range(CONV_W - 1):
            back = (CONV_W - 1 - jj) * stride
            cc = cc + ext_ref[halo - back:halo - back + tm, c0:c1] * cw_ref[jj:jj + 1, c0:c1]
        cc = cc + gate * cw_ref[CONV_W - 1:CONV_W, c0:c1]
        act = (0.5 * cc * (1.0 + lax.erf(cc * (2.0 ** -0.5))) * val).astype(BF16)
        return jnp.dot(act, wdn_ref[c0:c1, :], preferred_element_type=F32)

    nchunk = D_FF // FF_CHUNK
    acc = x
    pending = up(0)
    for c in range(nchunk):
        nxt = up(c + 1) if c + 1 < nchunk else None
        acc = acc + down(c, *pending)
        pending = nxt
    xo_ref[...] = _rms(acc, gout_ref[...], NORM_EPS) if out_norm else acc

    @pl.when(s == ns - 1)
    def _():
        cout_ref[...] = ext_ref[halo + tm - keep:halo + tm, :]

    ext_ref[halo - keep:halo, :] = ext_ref[halo + tm - keep:halo + tm, :]


def _ffn(x, gn, wup, cw, cb, wdn, li, cprev, *, tm, stride, out_gain=None):
    G, T, _ = x.shape
    keep = (CONV_W - 1) * stride
    halo = max(SUBLANES, keep)
    assert tm >= keep
    row = pl.BlockSpec((None, tm, D_MODEL), lambda g, s: (g, s, 0))
    st = pl.BlockSpec((None, keep, D_FF), lambda g, s: (g, 0, 0))
    out_norm = out_gain is not None
    extra_specs, extra_args = ([_const_spec((1, D_MODEL))], [out_gain]) if out_norm else ([], [])
    return pl.pallas_call(
        functools.partial(_ffn_kernel, tm=tm, stride=stride, halo=halo, out_norm=out_norm),
        grid=(G, T // tm),
        in_specs=[row, _const_spec((1, D_MODEL)), _layer_spec((D_MODEL, 2 * D_FF), li),
                  _const_spec((CONV_W, D_FF)), _const_spec((1, D_FF)), _layer_spec((D_FF, D_MODEL), li), st]
        + extra_specs,
        out_specs=[row, st],
        out_shape=[jax.ShapeDtypeStruct((G, T, D_MODEL), F32), jax.ShapeDtypeStruct((G, keep, D_FF), F32)],
        scratch_shapes=[pltpu.VMEM((halo + tm, D_FF), F32)],
        compiler_params=_cparams(("arbitrary", "arbitrary")), name="ffn",
    )(x, gn, wup, cw, cb, wdn, cprev, *extra_args)


def _to_wkv_sample(a, Bn):
    T = a.shape[0] // Bn
    a = a.reshape(T, Bn, B_HEADS, B_HD).transpose(0, 3, 1, 2).reshape(T, B_HD, Bn * B_HEADS // LANES, LANES)
    return a.transpose(2, 0, 1, 3)


def _o_from_wkv_sample(o, Bn):
    LG, T = o.shape[0], o.shape[1]
    o = o.transpose(1, 2, 0, 3).reshape(T, B_HD, Bn, B_HEADS).transpose(0, 2, 3, 1)
    return o.reshape(T * Bn, B_WIDTH)


def _state_to_wkv_sample(st):
    Bn = st.shape[0]
    st = st.transpose(2, 3, 0, 1).reshape(B_HD, B_HD, Bn * B_HEADS // LANES, LANES)
    return st.transpose(2, 0, 1, 3)


def _state_from_wkv_sample(st, Bn):
    st = st.transpose(1, 2, 0, 3).reshape(B_HD, B_HD, Bn, B_HEADS)
    return st.transpose(2, 3, 0, 1)


def _layer_params(W, e):
    z = jnp.zeros((DECAY_RANK, B_WIDTH), F32)
    head = jnp.arange(B_WIDTH) // B_HD
    same = (head[:, None] == head[None, :])
    return {
        "mu": W["rwkv_mu"][e][None], "w0": W["rwkv_w0"][e][None],
        "w2p": jnp.concatenate([W["rwkv_w2"][e], z], axis=0).astype(BF16),
        "a0": W["rwkv_a0"][e][None],
        "a2p": jnp.concatenate([z, W["rwkv_a2"][e]], axis=0).astype(BF16),
        "g2": W["rwkv_g2"][e].astype(BF16),
        "k_k": W["rwkv_k_k"][e][None], "k_a": W["rwkv_k_a"][e][None],
        "r_k": W["rwkv_r_k"][e].reshape(1, B_WIDTH),
        "seg": same.astype(BF16), "avg": (same.astype(F32) / B_HD).astype(BF16),
    }


def _trunk(x, W, *, prompt, wkv_in, shift_in, pool_in, conv_in, attn_ctx):
    G, T, _ = x.shape
    if prompt:
        Bn, stride, pos0 = G, 1, 0
        tm_in = tm_ffn = tm_pool = min(T, ROW_TILE)
    else:
        Bn, stride, pos0 = attn_ctx["batch"], attn_ctx["batch"], attn_ctx["past_len"]
        tm_in = tm_ffn = tm_pool = T
    N = G * T
    tm_flat = min(N, ROW_TILE)
    k_rows, v_rows, wkv_out, shift_out, pool_out, conv_out = [], [], [], [], [], []
    for l in range(DEPTH):
        gn = W["norm_mix"][l][None]
        if l % 2 == 0:
            e = l // 2
            lam_init = 0.8 - 0.6 * math.exp(-0.3 * l)
            prm = _layer_params(W, e)
            (k, v, qkv, s_new, nkk, d, b, kt, r, vb, g, bonus) = _inproj(
                x, gn, W["w_in_bf"], e, shift_in[e], prm, tm=tm_in, stride=stride)
            lq = W["diff_lambda"][e]
            sub = W["diff_subln"][e][None]
            if prompt:
                oa = _pattn(qkv, lq, sub, lam_init, tq=min(T, ATTN_TILE)).reshape(N, A_WIDTH)
                o, st = _wkv_chunked(nkk, d, b, kt, r, vb)
                o = o.reshape(N, B_WIDTH)
            else:
                nt = T // Bn
                bh = lambda a: a.reshape(nt, Bn, A_HEADS, 2 * A_HD).transpose(1, 2, 0, 3)
                padr = lambda a, rows: jnp.pad(a, ((0, 0), (0, 0), (0, rows - a.shape[2]), (0, 0)))
                q8 = padr(bh(qkv[0, :, :A_WIDTH]), SROWS // 2)
                comp = (jnp.arange(2 * A_HD) // A_HD)
                qh = jnp.concatenate([jnp.where(comp == c, q8, jnp.zeros((), BF16)) for c in range(2)], axis=2)
                oa = _sattn(attn_ctx["page_table"], qh, padr(bh(k[0]), NEW_KEY_ROWS), padr(bh(v[0]), NEW_KEY_ROWS),
                            lq, sub, attn_ctx["cache_k"], attn_ctx["cache_v"], e, nt, lam_init)
                oa = oa[:, :nt].transpose(1, 0, 2).reshape(N, A_WIDTH).astype(BF16)
                streams = [_to_wkv_sample(a[0], Bn) for a in (nkk, d, b, kt, r, vb)]
                o, st = _wkv(*streams, _state_to_wkv_sample(wkv_in[e]), tb=nt)
                o = _o_from_wkv_sample(o, Bn)
                st = _state_from_wkv_sample(st, Bn)
            x = _mixout(x.reshape(N, D_MODEL), oa, o, g.reshape(N, B_WIDTH), bonus.reshape(N, B_WIDTH),
                        W["rwkv_gn_w"][e][None], W["rwkv_gn_b"][e][None], prm["avg"], W["w_out_bf"], e,
                        tm=tm_flat).reshape(G, T, D_MODEL)
            k_rows.append(k)
            v_rows.append(v)
            wkv_out.append(st)
            shift_out.append(s_new)
        else:
            o_ = l // 2
            x, buf = _pool(x, gn, W["pool_w_bf"][o_], W["pool_scale"][o_][None], pool_in[o_],
                           tm=tm_pool, stride=stride, pos0=pos0)
            pool_out.append(buf)
        x, cbuf = _ffn(x, W["norm_ffn"][l][None], W["ffn_up_bf"], W["ffn_conv_w"][l],
                       W["ffn_conv_b"][l][None], W["ffn_down_bf"], l, conv_in[l], tm=tm_ffn, stride=stride,
                       out_gain=W["norm_final"][None] if l == DEPTH - 1 else None)
        conv_out.append(cbuf)
    return (x, jnp.stack(k_rows), jnp.stack(v_rows), jnp.stack(wkv_out), jnp.stack(shift_out),
            jnp.stack(pool_out), jnp.stack(conv_out))


def kernel(x_prompt, x_sample, cache_k, cache_v, state_wkv, state_shift, state_pool, state_ffn_conv,
           page_table, norm_mix, norm_ffn, norm_final, w_in, w_out, diff_lambda, diff_subln,
           rwkv_mu, rwkv_w0, rwkv_w2, rwkv_a0, rwkv_a2, rwkv_g2, rwkv_k_k, rwkv_k_a, rwkv_r_k,
           rwkv_gn_w, rwkv_gn_b, pool_w, pool_scale, ffn_up, ffn_conv_w, ffn_conv_b, ffn_down):
    W = dict(norm_mix=norm_mix, norm_ffn=norm_ffn, norm_final=norm_final,
             diff_lambda=diff_lambda, diff_subln=diff_subln, rwkv_mu=rwkv_mu, rwkv_w0=rwkv_w0,
             rwkv_w2=rwkv_w2, rwkv_a0=rwkv_a0, rwkv_a2=rwkv_a2, rwkv_g2=rwkv_g2, rwkv_k_k=rwkv_k_k,
             rwkv_k_a=rwkv_k_a, rwkv_r_k=rwkv_r_k, rwkv_gn_w=rwkv_gn_w, rwkv_gn_b=rwkv_gn_b,
             pool_scale=pool_scale, ffn_conv_w=ffn_conv_w, ffn_conv_b=ffn_conv_b,
             w_in_bf=w_in.astype(BF16), w_out_bf=w_out.astype(BF16), pool_w_bf=pool_w.astype(BF16),
             ffn_up_bf=ffn_up.astype(BF16), ffn_down_bf=ffn_down.astype(BF16))
    n_even, n_odd = state_wkv.shape[0], state_pool.shape[0]
    Bp, Sp, _ = x_prompt.shape
    Bs, Ts, _ = x_sample.shape
    n_pages = page_table.shape[1]
    past_len = n_pages * cache_k.shape[2]

    y_p, k_p, v_p, wkv_p, shift_p, pool_p, conv_p = _trunk(
        x_prompt, W, prompt=True, wkv_in=None,
        shift_in=jnp.zeros((n_even, Bp, 1, B_COLS), F32),
        pool_in=jnp.zeros((n_odd, Bp, POOL_BUF, D_MODEL), F32),
        conv_in=jnp.zeros((DEPTH, Bp, CONV_W - 1, D_FF), F32), attn_ctx=None)
    k_p = k_p.reshape(n_even, Bp, Sp, A_HEADS, 2 * A_HD)
    v_p = v_p.reshape(n_even, Bp, Sp, A_HEADS, 2 * A_HD)
    shift_p = shift_p.reshape(n_even, Bp, B_COLS)

    tmaj = lambda a: jnp.swapaxes(a, -3, -2)
    flat = lambda a: a.reshape(a.shape[:-3] + (1, a.shape[-3] * a.shape[-2], a.shape[-1]))
    ctx = dict(batch=Bs, past_len=past_len, page_table=page_table,
               cache_k=cache_k.reshape(cache_k.shape[:2] + (PAGE_SIZE * A_HEADS, 2 * A_HD)),
               cache_v=cache_v.reshape(cache_v.shape[:2] + (PAGE_SIZE * A_HEADS, 2 * A_HD)))
    y_s, k_s, v_s, wkv_s, shift_s, pool_s, conv_s = _trunk(
        flat(tmaj(x_sample)), W, prompt=False, wkv_in=state_wkv,
        shift_in=state_shift[:, None], pool_in=flat(tmaj(state_pool)),
        conv_in=flat(tmaj(state_ffn_conv)), attn_ctx=ctx)
    unflat = lambda a, j: tmaj(a.reshape(a.shape[:-3] + (j, Bs, a.shape[-1])))
    y_s = unflat(y_s, Ts)
    k_s = k_s.reshape(n_even, Ts, Bs, A_HEADS, 2 * A_HD).transpose(0, 2, 1, 3, 4)
    v_s = v_s.reshape(n_even, Ts, Bs, A_HEADS, 2 * A_HD).transpose(0, 2, 1, 3, 4)
    shift_s = shift_s.reshape(n_even, Bs, B_COLS)
    pool_s = unflat(pool_s, POOL_BUF)
    conv_s = unflat(conv_s, CONV_W - 1)
    return (y_p, y_s, k_p, v_p, wkv_p, shift_p, pool_p, conv_p,
            k_s, v_s, wkv_s, shift_s, pool_s, conv_s)
```

```python
import functools
import math

import jax
import jax.numpy as jnp
from jax import lax
from jax.experimental import pallas as pl
from jax.experimental.pallas import tpu as pltpu

F32 = jnp.float32
BF16 = jnp.bfloat16

D_MODEL = 1024
DEPTH = 4
PAGE_SIZE = 128
A_HD = 64
A_HEADS = 4
A_WIDTH = 512
A_COLS = 3 * A_WIDTH
A_SCALE = A_HD ** -0.5
Q_SCALE = A_SCALE * math.log2(math.e)
B_HD = 64
B_WIDTH = 512
B_HEADS = 8
DECAY_RANK = 64
ICLR_RANK = 64
GATE_RANK = 128
B_COLS = 3 * B_WIDTH + DECAY_RANK + ICLR_RANK + GATE_RANK
IN_COLS = A_COLS + B_COLS
POOL_WINDOWS = (2, 4, 8, 16)
POOL_GW = D_MODEL // len(POOL_WINDOWS)
POOL_BUF = max(POOL_WINDOWS) - 1
D_FF = 2816
CONV_W = 3
NORM_EPS = 1e-6
SUBLN_EPS = 1e-5
GN_EPS = 64e-5

LANES = 128
SUBLANES = 8
VMEM_LIMIT = 56 * 1024 * 1024
ROW_TILE = 512
ATTN_TILE = 1024
FF_CHUNK = 2816
PAGES_PER_STEP = 32
NEW_KEY_ROWS = 16
SROWS = 16
HEAD_LOG2 = 6


def _div_nonneg(x, n):
    if n == 1:
        return x
    if n & (n - 1) == 0:
        return lax.shift_right_logical(x, n.bit_length() - 1)
    return x // n


def _cparams(sem):
    return pltpu.CompilerParams(dimension_semantics=sem, vmem_limit_bytes=VMEM_LIMIT)


def _const_spec(shape):
    nd = len(shape)
    return pl.BlockSpec(shape, lambda *_: (0,) * nd, pipeline_mode=pl.Buffered(1))


def _layer_spec(shape, li):
    nd = len(shape)
    return pl.BlockSpec((None,) + tuple(shape), lambda *_: (li,) + (0,) * nd, pipeline_mode=pl.Buffered(1))


def _rms(x, g, eps):
    return x * lax.rsqrt(jnp.mean(x * x, axis=-1, keepdims=True) + eps) * g


def _bf16_dot(x, m):
    return jnp.dot(x.astype(BF16), m, preferred_element_type=F32)


def _lane_tile(x, n):
    return jnp.concatenate([x] * n, axis=1)


def _softplus(z):
    return jnp.maximum(z, 0.0) + jnp.log1p(jnp.exp(-jnp.abs(z)))


def _inproj_kernel(x_ref, gn_ref, win_ref, sprev_ref, mu_ref, w0_ref, w2p_ref, a0_ref, a2p_ref,
                   g2_ref, kk_ref, ka_ref, rk_ref, seg_ref,
                   k_ref, v_ref, qkv_ref, sout_ref, nkk_ref, d_ref, b_ref, kt_ref, r_ref,
                   vb_ref, g_ref, bonus_ref, ext_ref, *, tm, stride, halo):
    s = pl.program_id(1)
    ns = pl.num_programs(1)
    h = _rms(x_ref[...], gn_ref[...], NORM_EPS).astype(BF16)
    p = jnp.dot(h, win_ref[...], preferred_element_type=F32)
    for hh in range(A_HEADS):
        c0 = A_WIDTH + hh * 2 * A_HD
        k_ref[pl.ds(hh, tm, stride=A_HEADS), :] = p[:, c0:c0 + 2 * A_HD]
        v_ref[pl.ds(hh, tm, stride=A_HEADS), :] = p[:, A_WIDTH + c0:A_WIDTH + c0 + 2 * A_HD]
    qkv_ref[:, :A_WIDTH] = (p[:, :A_WIDTH] * Q_SCALE).astype(BF16)
    qkv_ref[:, A_WIDTH:] = p[:, A_WIDTH:A_COLS].astype(BF16)
    pb = p[:, A_COLS:]

    @pl.when(s == 0)
    def _():
        ext_ref[halo - stride:halo, :] = sprev_ref[...]

    ext_ref[halo:halo + tm, :] = pb
    prev = ext_ref[halo - stride:halo - stride + tm, :]
    ext_ref[halo - stride:halo, :] = ext_ref[halo + tm - stride:halo + tm, :]

    @pl.when(s == ns - 1)
    def _():
        sout_ref[...] = ext_ref[halo + tm - stride:halo + tm, :]

    xm = pb + (prev - pb) * mu_ref[...]
    r = xm[:, :B_WIDTH]
    k = xm[:, B_WIDTH:2 * B_WIDTH]
    v = xm[:, 2 * B_WIDTH:3 * B_WIDTH]
    wa = xm[:, 3 * B_WIDTH:3 * B_WIDTH + DECAY_RANK + ICLR_RANK]
    gd = xm[:, 3 * B_WIDTH + DECAY_RANK + ICLR_RANK:]
    wlin = w0_ref[...] + jnp.dot(jnp.tanh(wa).astype(BF16), w2p_ref[...], preferred_element_type=F32)
    w = -_softplus(-wlin) - 0.5
    log_decay = -jnp.exp(w)
    a = jax.nn.sigmoid(a0_ref[...] + jnp.dot(wa.astype(BF16), a2p_ref[...], preferred_element_type=F32))
    g = jnp.dot(jax.nn.sigmoid(gd).astype(BF16), g2_ref[...], preferred_element_type=F32)
    kk = k * kk_ref[...]
    ss = _bf16_dot(kk * kk, seg_ref[...])
    kk = kk / jnp.maximum(jnp.sqrt(ss), 1e-12)
    kt = k * (1.0 + (a - 1.0) * ka_ref[...])
    bonus = _bf16_dot(r * kt * rk_ref[...], seg_ref[...]) * v
    nkk_ref[...] = -kk
    d_ref[...] = log_decay
    b_ref[...] = kk * a
    kt_ref[...] = kt
    r_ref[...] = r
    vb_ref[...] = v
    g_ref[...] = g
    bonus_ref[...] = bonus


def _inproj(x, gn, win, li, sprev, prm, *, tm, stride):
    G, T, _ = x.shape
    halo = max(SUBLANES, stride)
    ns = T // tm
    row = lambda w: pl.BlockSpec((None, tm, w), lambda g, s: (g, s, 0))
    stream = jax.ShapeDtypeStruct((G, T, B_WIDTH), F32)
    in_specs = [row(D_MODEL), _const_spec((1, D_MODEL)), _layer_spec((D_MODEL, IN_COLS), li),
                pl.BlockSpec((None, stride, B_COLS), lambda g, s: (g, 0, 0)),
                _const_spec((1, B_COLS)), _const_spec((1, B_WIDTH)), _const_spec((LANES, B_WIDTH)),
                _const_spec((1, B_WIDTH)), _const_spec((LANES, B_WIDTH)), _const_spec((GATE_RANK, B_WIDTH)),
                _const_spec((1, B_WIDTH)), _const_spec((1, B_WIDTH)), _const_spec((1, B_WIDTH)),
                _const_spec((B_WIDTH, B_WIDTH))]
    kv_shape = jax.ShapeDtypeStruct((G, T * A_HEADS, 2 * A_HD), F32)
    kv_spec = pl.BlockSpec((None, tm * A_HEADS, 2 * A_HD), lambda g, s: (g, s, 0))
    out_shape = [kv_shape, kv_shape, jax.ShapeDtypeStruct((G, T, A_COLS), BF16),
                 jax.ShapeDtypeStruct((G, stride, B_COLS), F32)] + [stream] * 8
    out_specs = [kv_spec, kv_spec, row(A_COLS),
                 pl.BlockSpec((None, stride, B_COLS), lambda g, s: (g, 0, 0))] + [row(B_WIDTH)] * 8
    return pl.pallas_call(
        functools.partial(_inproj_kernel, tm=tm, stride=stride, halo=halo),
        grid=(G, ns), in_specs=in_specs, out_specs=out_specs, out_shape=out_shape,
        scratch_shapes=[pltpu.VMEM((halo + tm, B_COLS), F32)],
        compiler_params=_cparams(("arbitrary", "arbitrary")), name="inproj",
    )(x, gn, win, sprev, prm["mu"], prm["w0"], prm["w2p"], prm["a0"], prm["a2p"], prm["g2"],
      prm["k_k"], prm["k_a"], prm["r_k"], prm["seg"])


def _lambda_full(lq, lam_init):
    s01 = jnp.sum(lq[0:1, :] * lq[1:2, :], axis=-1, keepdims=True)
    s23 = jnp.sum(lq[2:3, :] * lq[3:4, :], axis=-1, keepdims=True)
    return jnp.exp(s01) - jnp.exp(s23) + lam_init


def _pattn_kernel(qi_ref, kj_ref, q_ref, k_ref, v_ref, lq_ref, sub_ref, o_ref,
                  m0_ref, l0_ref, a0_ref, m1_ref, l1_ref, a1_ref, *, tq, lam_init):
    qi = qi_ref[pl.program_id(2)]
    kj = kj_ref[pl.program_id(2)]

    @pl.when(kj == 0)
    def _():
        for m_ref, l_ref, a_ref in ((m0_ref, l0_ref, a0_ref), (m1_ref, l1_ref, a1_ref)):
            m_ref[...] = jnp.full(m_ref.shape, -jnp.inf, F32)
            l_ref[...] = jnp.zeros(l_ref.shape, F32)
            a_ref[...] = jnp.zeros(a_ref.shape, F32)

    def accumulate(diagonal):
        q = q_ref[...]
        k = k_ref[...]
        v = v_ref[...]
        lane = lax.broadcasted_iota(jnp.int32, q.shape, 1)
        zero = jnp.zeros_like(q)
        if diagonal:
            keep = (lax.broadcasted_iota(jnp.int32, (tq, tq), 1)
                    <= lax.broadcasted_iota(jnp.int32, (tq, tq), 0))
        refs = ((m0_ref, l0_ref, a0_ref), (m1_ref, l1_ref, a1_ref))
        comps = range(2)
        qc = [jnp.where((lane >= c * A_HD) & (lane < (c + 1) * A_HD), q, zero) for c in comps]
        s = [lax.dot_general(qc[c], k, (((1,), (1,)), ((), ())), preferred_element_type=F32) for c in comps]
        if diagonal:
            s = [jnp.where(keep, s[c], -jnp.inf) for c in comps]
        m_old = [refs[c][0][...] for c in comps]
        m_new = [jnp.maximum(m_old[c], jnp.max(s[c], axis=-1, keepdims=True)) for c in comps]
        alpha = [jnp.exp2(m_old[c] - m_new[c]) for c in comps]
        p = [jnp.exp2(s[c] - _lane_tile(m_new[c], tq // LANES)) for c in comps]
        for c in comps:
            m_ref, l_ref, a_ref = refs[c]
            l_ref[...] = alpha[c] * l_ref[...] + jnp.sum(p[c], axis=-1, keepdims=True)
            a_ref[...] = alpha[c] * a_ref[...] + jnp.dot(p[c].astype(BF16), v, preferred_element_type=F32)
            m_ref[...] = m_new[c]

    @pl.when(kj < qi)
    def _():
        accumulate(False)

    @pl.when(kj == qi)
    def _():
        accumulate(True)
        lam = _lambda_full(lq_ref[...], lam_init)
        o = a0_ref[...] / l0_ref[...] - lam * (a1_ref[...] / l1_ref[...])
        o = _rms(o, sub_ref[...], SUBLN_EPS) * (1.0 - lam_init)
        o_ref[...] = o.astype(o_ref.dtype)


def _pattn(qkv, lq, sub, lam_init, *, tq):
    B, S, _ = qkv.shape
    nq = S // tq
    nh = A_HEADS
    pairs = [(i, j) for i in range(nq) for j in range(i + 1)]
    qi_tab = jnp.asarray([p[0] for p in pairs], jnp.int32)
    kj_tab = jnp.asarray([p[1] for p in pairs], jnp.int32)
    q_spec = pl.BlockSpec((None, tq, LANES), lambda b, h, n, qt, kt: (b, qt[n], h))
    k_spec = pl.BlockSpec((None, tq, LANES), lambda b, h, n, qt, kt: (b, kt[n], nh + h))
    v_spec = pl.BlockSpec((None, tq, LANES), lambda b, h, n, qt, kt: (b, kt[n], 2 * nh + h))
    grid_spec = pltpu.PrefetchScalarGridSpec(
        num_scalar_prefetch=2, grid=(B, nh, len(pairs)),
        in_specs=[q_spec, k_spec, v_spec,
                  pl.BlockSpec((4, A_HD), lambda b, h, n, qt, kt: (0, 0)),
                  pl.BlockSpec((1, LANES), lambda b, h, n, qt, kt: (0, 0))],
        out_specs=pl.BlockSpec((None, tq, LANES), lambda b, h, n, qt, kt: (b, qt[n], h)),
        scratch_shapes=[pltpu.VMEM((tq, LANES), F32)] * 6)
    return pl.pallas_call(
        functools.partial(_pattn_kernel, tq=tq, lam_init=lam_init),
        grid_spec=grid_spec, out_shape=jax.ShapeDtypeStruct((B, S, A_WIDTH), BF16),
        compiler_params=_cparams(("arbitrary",) * 3), name="pattn",
    )(qi_tab, kj_tab, qkv, qkv, qkv, lq, sub)


def _sattn_kernel(pt_ref, q_ref, kn_ref, vn_ref, lq_ref, sub_ref, *rest, npg, nt, lam_init):
    k_refs = rest[:npg]
    v_refs = rest[npg:2 * npg]
    o_ref = rest[2 * npg]
    m_ref, l_ref, acc_ref = rest[2 * npg + 1:]
    j = pl.program_id(1)
    nj = pl.num_programs(1)

    @pl.when(j == 0)
    def _():
        m_ref[...] = jnp.full(m_ref.shape, -jnp.inf, F32)
        l_ref[...] = jnp.zeros(l_ref.shape, F32)
        acc_ref[...] = jnp.zeros(acc_ref.shape, F32)

    nt_dims = (((1,), (1,)), ((), ()))
    qs = [q_ref[h] for h in range(A_HEADS)]

    def head_rows(pg_ref, h):
        return pg_ref[pl.ds(h, PAGE_SIZE, stride=A_HEADS), :].astype(BF16)

    def update(s, pv):
        m_old = m_ref[...]
        m_new = jnp.maximum(m_old, jnp.max(s, axis=-1, keepdims=True))
        alpha = jnp.exp2(m_old - m_new)
        p = jnp.exp2(s - (_lane_tile(m_new, s.shape[1] // LANES) if s.shape[1] > LANES
                          else m_new[:, :s.shape[1]]))
        l_ref[...] = alpha * l_ref[...] + jnp.sum(p, axis=-1, keepdims=True)
        acc_ref[...] = alpha * acc_ref[...] + pv(p.astype(BF16))
        m_ref[...] = m_new

    s_heads = [jnp.concatenate(
        [lax.dot_general(qs[h], head_rows(kr, h), nt_dims, preferred_element_type=F32) for kr in k_refs],
        axis=1) for h in range(A_HEADS)]

    def pv_pages(p):
        outs = []
        for h in range(A_HEADS):
            ph = p[h * SROWS:(h + 1) * SROWS, :]
            tot = None
            for i, vr in enumerate(v_refs):
                part = jnp.dot(ph[:, i * PAGE_SIZE:(i + 1) * PAGE_SIZE], head_rows(vr, h),
                               preferred_element_type=F32)
                tot = part if tot is None else tot + part
            outs.append(tot)
        return jnp.concatenate(outs, axis=0)

    update(jnp.concatenate(s_heads, axis=0), pv_pages)

    @pl.when(j == nj - 1)
    def _():
        s_new = jnp.concatenate(
            [lax.dot_general(qs[h], kn_ref[h].astype(BF16), nt_dims, preferred_element_type=F32)
             for h in range(A_HEADS)], axis=0)
        key = lax.broadcasted_iota(jnp.int32, s_new.shape, 1)
        tok = lax.broadcasted_iota(jnp.int32, s_new.shape, 0) & (SROWS // 2 - 1)
        s_new = jnp.where((key <= tok) & (key < nt), s_new, -jnp.inf)

        def pv_new(p):
            return jnp.concatenate(
                [jnp.dot(p[h * SROWS:(h + 1) * SROWS, :], vn_ref[h].astype(BF16), preferred_element_type=F32)
                 for h in range(A_HEADS)], axis=0)

        update(s_new, pv_new)
        lam = _lambda_full(lq_ref[...], lam_init)
        half = SROWS // 2
        outs = []
        for h in range(A_HEADS):
            r0 = h * SROWS
            o = (acc_ref[r0:r0 + half, :] / l_ref[r0:r0 + half, :]
                 - lam * (acc_ref[r0 + half:r0 + SROWS, :] / l_ref[r0 + half:r0 + SROWS, :]))
            outs.append(_rms(o, sub_ref[...], SUBLN_EPS) * (1.0 - lam_init))
        o_ref[...] = jnp.concatenate(outs, axis=-1)


def _sattn(page_table, qh, k_new, v_new, lq, sub, kc, vc, e, nt, lam_init):
    Bn = qh.shape[0]
    n_pages = page_table.shape[1]
    npg = PAGES_PER_STEP
    nj = n_pages // npg
    page_rows = PAGE_SIZE * A_HEADS
    per_b = lambda r: pl.BlockSpec((None, A_HEADS, r, LANES), lambda b, j, pt: (b, 0, 0, 0))

    def page_spec(i):
        return pl.BlockSpec((None, None, page_rows, LANES), lambda b, j, pt: (e, pt[b, j * npg + i], 0, 0))

    in_specs = ([per_b(SROWS), per_b(NEW_KEY_ROWS), per_b(NEW_KEY_ROWS),
                 pl.BlockSpec((4, A_HD), lambda b, j, pt: (0, 0)),
                 pl.BlockSpec((1, LANES), lambda b, j, pt: (0, 0))]
                + [page_spec(i) for i in range(npg)] * 2)
    nr = A_HEADS * SROWS
    grid_spec = pltpu.PrefetchScalarGridSpec(
        num_scalar_prefetch=1, grid=(Bn, nj), in_specs=in_specs,
        out_specs=pl.BlockSpec((None, SROWS // 2, A_WIDTH), lambda b, j, pt: (b, 0, 0)),
        scratch_shapes=[pltpu.VMEM((nr, LANES), F32)] * 3)
    return pl.pallas_call(
        functools.partial(_sattn_kernel, npg=npg, nt=nt, lam_init=lam_init),
        grid_spec=grid_spec, out_shape=jax.ShapeDtypeStruct((Bn, SROWS // 2, A_WIDTH), F32),
        compiler_params=_cparams(("arbitrary", "arbitrary")), name="sattn",
    )(page_table, qh, k_new, v_new, lq, sub, *([kc] * npg), *([vc] * npg))


def _wkv_kernel(nkk_ref, d_ref, b_ref, kt_ref, r_ref, v_ref, s0_ref, o_ref, st_ref, s_scr, *, tb, nv):
    t0 = pl.program_id(1)

    @pl.when(t0 == 0)
    def _():
        s_scr[...] = s0_ref[...]

    def step(t, carry):
        nkk = nkk_ref[t]
        d = jnp.exp(d_ref[t])
        b = b_ref[t]
        kt = kt_ref[t]
        r = r_ref[t]
        dr = d * r
        br = jnp.sum(b * r, axis=0, keepdims=True)
        kr = jnp.sum(kt * r, axis=0, keepdims=True)

        def row(vr, c):
            S = s_scr[vr]
            sa = jnp.sum(S * nkk, axis=0, keepdims=True)
            qo = jnp.sum(S * dr, axis=0, keepdims=True)
            vv = v_ref[t, pl.ds(vr, 1), :]
            s_scr[vr] = S * d + sa * b + vv * kt
            o_ref[t, pl.ds(vr, 1), :] = qo + sa * br + vv * kr
            return c

        return lax.fori_loop(0, nv, row, carry, unroll=2)

    lax.fori_loop(0, tb, step, 0)
    st_ref[...] = s_scr[...]


def _wkv(nkk, d, b, kt, r, v, s0, *, tb):
    LG, T, _, _ = nkk.shape
    nv = v.shape[2]
    kspec = pl.BlockSpec((None, tb, B_HD, LANES), lambda g, t: (g, t, 0, 0))
    vspec = pl.BlockSpec((None, tb, nv, LANES), lambda g, t: (g, t, 0, 0))
    sspec = pl.BlockSpec((None, nv, B_HD, LANES), lambda g, t: (g, 0, 0, 0))
    return pl.pallas_call(
        functools.partial(_wkv_kernel, tb=tb, nv=nv),
        grid=(LG, T // tb), in_specs=[kspec] * 5 + [vspec, sspec],
        out_specs=[vspec, sspec],
        out_shape=[jax.ShapeDtypeStruct((LG, T, nv, LANES), F32),
                   jax.ShapeDtypeStruct((LG, nv, B_HD, LANES), F32)],
        scratch_shapes=[pltpu.VMEM((nv, B_HD, LANES), F32)],
        compiler_params=_cparams(("arbitrary", "arbitrary")), name="wkv",
    )(nkk, d, b, kt, r, v, s0)


WKV_CHUNK = 64
WKV_GROUP = 4


def _split3_dot(m, x):
    x1 = x.astype(BF16)
    r1 = x - x1.astype(F32)
    x2 = r1.astype(BF16)
    x3 = (r1 - x2.astype(F32)).astype(BF16)
    return (jnp.dot(m, x1, preferred_element_type=F32) + jnp.dot(m, x2, preferred_element_type=F32)
            + jnp.dot(m, x3, preferred_element_type=F32))


def _wkv_chunk_math(a, ld, b, kt, r, v, s):
    L, HW = WKV_CHUNK, WKV_GROUP * B_HD
    nt_dims = (((1,), (1,)), ((), ()))
    tn_dims = (((0,), (0,)), ((), ()))
    row = lax.broadcasted_iota(jnp.int32, (L, HW), 0)
    col = lax.broadcasted_iota(jnp.int32, (L, HW), 1) & (B_HD - 1)
    strict = col < row
    incl = col <= row
    blk = lax.shift_right_logical
    bd_mask = (lax.shift_right_logical(lax.broadcasted_iota(jnp.int32, (HW, HW), 0), HEAD_LOG2)
               == lax.shift_right_logical(lax.broadcasted_iota(jnp.int32, (HW, HW), 1), HEAD_LOG2))

    def bd(y):
        y = y.astype(BF16)
        return jnp.where(bd_mask, jnp.concatenate([y] * WKV_GROUP, axis=0), jnp.zeros((), BF16))

    def hprod(x, y):
        return jnp.dot(x.astype(BF16), bd(y), preferred_element_type=F32)

    def each(f, *lists):
        return [f(*xs) for xs in zip(*lists)]

    def dotg(dims):
        return lambda x, y: lax.dot_general(x, y, dims, preferred_element_type=F32)

    tri = (lax.broadcasted_iota(jnp.int32, (L, L), 1) <= lax.broadcasted_iota(jnp.int32, (L, L), 0)).astype(BF16)
    logp = each(lambda x: _split3_dot(tri, x), ld)
    ep = each(jnp.exp, logp)
    p_last = each(lambda x: x[L - 1:L, :], ep)
    en = each(lambda x: jnp.exp(-x), logp)
    at = each(lambda x, lp, l_: x * jnp.exp(lp - l_), a, logp, ld)
    rt = each(jnp.multiply, r, ep)
    bt = each(jnp.multiply, b, en)
    ktt = each(jnp.multiply, kt, en)
    bh = each(lambda x, pp: (x * pp).astype(BF16), bt, p_last)
    kh = each(lambda x, pp: (x * pp).astype(BF16), ktt, p_last)

    lhs = each(lambda x, y: jnp.concatenate([x, y], axis=0).astype(BF16), at, rt)
    xb = each(lambda x, y: dotg(nt_dims)(x, bd(y)), lhs, bt)
    xk = each(lambda x, y: dotg(nt_dims)(x, bd(y)), lhs, ktt)
    n = each(lambda x: jnp.where(strict, x[:L], 0.0), xb)
    arb = each(lambda x: jnp.where(incl, x[L:], 0.0), xb)
    aak = each(lambda x: jnp.where(strict, x[:L], 0.0), xk)
    ark = each(lambda x: jnp.where(incl, x[L:], 0.0), xk)

    eye = jnp.where(col == row, 1.0, 0.0)
    tm = each(lambda x: eye + jnp.where(blk(row, 1) == blk(col, 1), x, 0.0), n)
    for lg in range(1, HEAD_LOG2):
        sel = (blk(row, lg + 1) == blk(col, lg + 1)) & (blk(row, lg) != blk(col, lg))
        w = each(lambda t_, x: hprod(t_, jnp.where(sel, x, 0.0)), tm, n)
        tm = each(lambda t_, w_: t_ + hprod(w_, t_), tm, w)

    rloc = each(hprod, aak, v)
    uloc = each(hprod, tm, rloc)
    ah = each(hprod, tm, at)
    ro = each(lambda x, y, z: (x + hprod(y, z)).astype(BF16), rt, arb, ah)
    oc = each(lambda x, u, y, vv: hprod(x, u) + hprod(y, vv), arb, uloc, ark, v)
    mpp = each(lambda x, y: jnp.where(bd_mask, dotg(tn_dims)(x.astype(BF16), y), 0.0).astype(BF16), ah, bh)
    sc = each(lambda u, y, vv, z: jnp.where(
        bd_mask, dotg(tn_dims)(u.astype(BF16), y) + dotg(tn_dims)(vv.astype(BF16), z), 0.0), uloc, bh, v, kh)

    s_hi = each(lambda x: x.astype(BF16), s)
    s_lo = each(lambda x, h: (x - h.astype(F32)).astype(BF16), s, s_hi)
    o = each(lambda q, h, c_: dotg(nt_dims)(q, h) + c_, ro, s_hi, oc)
    s_new = each(lambda x, pp, h, l_, m_, c_: (x * pp + jnp.dot(h, m_, preferred_element_type=F32)
                                               + jnp.dot(l_, m_, preferred_element_type=F32) + c_),
                 s, p_last, s_hi, s_lo, mpp, sc)
    return o, s_new


def _wkv_chunk_kernel(a_ref, ld_ref, b_ref, kt_ref, r_ref, v_ref, o_ref, st_ref, s_scr, *, nb):
    c = pl.program_id(1)
    nc = pl.num_programs(1)
    HW = WKV_GROUP * B_HD

    @pl.when(c == 0)
    def _():
        s_scr[...] = jnp.zeros(s_scr.shape, F32)

    groups = [(bb, g) for bb in range(nb) for g in range(B_WIDTH // HW)]
    sls = [(bb, slice(None), slice(g * HW, (g + 1) * HW)) for bb, g in groups]
    o, s_new = _wkv_chunk_math(*[[ref[sl] for sl in sls] for ref in (a_ref, ld_ref, b_ref, kt_ref, r_ref, v_ref)],
                               [s_scr[bb, g] for bb, g in groups])
    for sl, (bb, g), o_i, s_i in zip(sls, groups, o, s_new):
        o_ref[sl] = o_i
        s_scr[bb, g] = s_i

    @pl.when(c == nc - 1)
    def _():
        st_ref[...] = s_scr[...]


def _wkv_chunked(a, ld, b, kt, r, v, *, nb=4):
    Bn, T, _ = a.shape
    L, HW = WKV_CHUNK, WKV_GROUP * B_HD
    ng = B_WIDTH // HW
    nb = min(nb, Bn)
    assert Bn % nb == 0 and T % L == 0
    spec = pl.BlockSpec((nb, L, B_WIDTH), lambda bb, c: (bb, c, 0))
    st_spec = pl.BlockSpec((nb, ng, HW, HW), lambda bb, c: (bb, 0, 0, 0))
    o, st = pl.pallas_call(
        functools.partial(_wkv_chunk_kernel, nb=nb),
        grid=(Bn // nb, T // L), in_specs=[spec] * 6, out_specs=[spec, st_spec],
        out_shape=[jax.ShapeDtypeStruct((Bn, T, B_WIDTH), F32), jax.ShapeDtypeStruct((Bn, ng, HW, HW), F32)],
        scratch_shapes=[pltpu.VMEM((nb, ng, HW, HW), F32)],
        compiler_params=_cparams(("arbitrary",) * 2), name="wkv_chunked",
    )(a, ld, b, kt, r, v)
    st = st.reshape(Bn, ng, WKV_GROUP, B_HD, WKV_GROUP, B_HD)
    st = jnp.stack([st[:, :, i, :, i, :] for i in range(WKV_GROUP)], axis=2)
    return o, st.reshape(Bn, B_HEADS, B_HD, B_HD)


def _mixout_kernel(x_ref, oa_ref, o_ref, g_ref, bonus_ref, gnw_ref, gnb_ref, avg_ref, wout_ref, xo_ref):
    o = o_ref[...]
    mu = _bf16_dot(o, avg_ref[...])
    dlt = o - mu
    var = _bf16_dot(dlt * dlt, avg_ref[...])
    on = dlt * lax.rsqrt(var + GN_EPS) * gnw_ref[...] + gnb_ref[...]
    ob = ((on + bonus_ref[...]) * g_ref[...]).astype(BF16)
    y = (jnp.dot(oa_ref[...], wout_ref[0:A_WIDTH, :], preferred_element_type=F32)
         + jnp.dot(ob, wout_ref[A_WIDTH:, :], preferred_element_type=F32))
    xo_ref[...] = x_ref[...] + y


def _mixout(x, oa, o, g, bonus, gnw, gnb, avg, wout, li, *, tm):
    N = x.shape[0]
    row = lambda w: pl.BlockSpec((tm, w), lambda i: (i, 0))
    return pl.pallas_call(
        _mixout_kernel, grid=(N // tm,),
        in_specs=[row(D_MODEL), row(A_WIDTH), row(B_WIDTH), row(B_WIDTH), row(B_WIDTH),
                  _const_spec((1, B_WIDTH)), _const_spec((1, B_WIDTH)),
                  _const_spec((B_WIDTH, B_WIDTH)), _layer_spec((D_MODEL, D_MODEL), li)],
        out_specs=row(D_MODEL), out_shape=jax.ShapeDtypeStruct((N, D_MODEL), F32),
        compiler_params=_cparams(("arbitrary",)), name="mixout",
    )(x, oa, o, g, bonus, gnw, gnb, avg, wout)


def _pool_kernel(x_ref, gn_ref, pw_ref, ps_ref, pprev_ref, xo_ref, pout_ref, ext_ref,
                 *, tm, stride, halo, pos0):
    s = pl.program_id(1)
    ns = pl.num_programs(1)
    keep = POOL_BUF * stride
    x = x_ref[...]
    h = _rms(x, gn_ref[...], NORM_EPS)

    @pl.when(s == 0)
    def _():
        ext_ref[halo - keep:halo, :] = pprev_ref[...]

    ext_ref[halo:halo + tm, :] = h
    t = _div_nonneg(s * tm + lax.broadcasted_iota(jnp.int32, (tm, 1), 0), stride)
    ys = []
    for gi, win in enumerate(POOL_WINDOWS):
        c0, c1 = gi * POOL_GW, (gi + 1) * POOL_GW
        cur = ext_ref[halo:halo + tm, c0:c1]
        wsum = cur
        for jj in range(1, win):
            wsum = wsum + ext_ref[halo - jj * stride:halo - jj * stride + tm, c0:c1]
        cnt = jnp.minimum(pos0 + t + 1, win).astype(F32)
        m = wsum / cnt - cur
        ys.append(jnp.dot(m.astype(BF16), pw_ref[gi], preferred_element_type=F32))
    xo_ref[...] = x + jnp.concatenate(ys, axis=-1) * ps_ref[...]

    @pl.when(s == ns - 1)
    def _():
        pout_ref[...] = ext_ref[halo + tm - keep:halo + tm, :]

    if tm >= keep:
        ext_ref[halo - keep:halo, :] = ext_ref[halo + tm - keep:halo + tm, :]


def _pool(x, gn, pw, ps, pprev, *, tm, stride, pos0):
    G, T, _ = x.shape
    halo = 2 * SUBLANES * stride
    keep = POOL_BUF * stride
    assert T // tm == 1 or tm >= keep
    row = pl.BlockSpec((None, tm, D_MODEL), lambda g, s: (g, s, 0))
    st = pl.BlockSpec((None, keep, D_MODEL), lambda g, s: (g, 0, 0))
    return pl.pallas_call(
        functools.partial(_pool_kernel, tm=tm, stride=stride, halo=halo, pos0=pos0),
        grid=(G, T // tm),
        in_specs=[row, _const_spec((1, D_MODEL)), _const_spec((len(POOL_WINDOWS), POOL_GW, POOL_GW)),
                  _const_spec((1, D_MODEL)), st],
        out_specs=[row, st],
        out_shape=[jax.ShapeDtypeStruct((G, T, D_MODEL), F32), jax.ShapeDtypeStruct((G, keep, D_MODEL), F32)],
        scratch_shapes=[pltpu.VMEM((halo + tm, D_MODEL), F32)],
        compiler_params=_cparams(("arbitrary", "arbitrary")), name="pool",
    )(x, gn, pw, ps, pprev)


def _ffn_kernel(x_ref, gn_ref, wup_ref, cw_ref, cb_ref, wdn_ref, cprev_ref, *rest, tm, stride, halo, out_norm):
    gout_ref = rest[0] if out_norm else None
    xo_ref, cout_ref, ext_ref = rest[-3:]
    s = pl.program_id(1)
    ns = pl.num_programs(1)
    keep = (CONV_W - 1) * stride
    x = x_ref[...]
    h = _rms(x, gn_ref[...], NORM_EPS).astype(BF16)

    @pl.when(s == 0)
    def _():
        ext_ref[halo - keep:halo, :] = cprev_ref[...]

    def up(c):
        c0, c1 = c * FF_CHUNK, (c + 1) * FF_CHUNK
        gate = jnp.dot(h, wup_ref[:, c0:c1], preferred_element_type=F32)
        val = jnp.dot(h, wup_ref[:, D_FF + c0:D_FF + c1], preferred_element_type=F32)
        ext_ref[halo:halo + tm, c0:c1] = gate
        return gate, val

    def down(c, gate, val):
        c0, c1 = c * FF_CHUNK, (c + 1) * FF_CHUNK
        cc = cb_ref[:, c0:c1]
        for jj in range(CONV_W - 1):
            back = (CONV_W - 1 - jj) * stride
            cc = cc + ext_ref[halo - back:halo - back + tm, c0:c1] * cw_ref[jj:jj + 1, c0:c1]
        cc = cc + gate * cw_ref[CONV_W - 1:CONV_W, c0:c1]
        act = (0.5 * cc * (1.0 + lax.erf(cc * (2.0 ** -0.5))) * val).astype(BF16)
        return jnp.dot(act, wdn_ref[c0:c1, :], preferred_element_type=F32)

    nchunk = D_FF // FF_CHUNK
    acc = x
    pending = up(0)
    for c in range(nchunk):
        nxt = up(c + 1) if c + 1 < nchunk else None
        acc = acc + down(c, *pending)
        pending = nxt
    xo_ref[...] = _rms(acc, gout_ref[...], NORM_EPS) if out_norm else acc

    @pl.when(s == ns - 1)
    def _():
        cout_ref[...] = ext_ref[halo + tm - keep:halo + tm, :]

    ext_ref[halo - keep:halo, :] = ext_ref[halo + tm - keep:halo + tm, :]


def _ffn(x, gn, wup, cw, cb, wdn, li, cprev, *, tm, stride, out_gain=None):
    G, T, _ = x.shape
    keep = (CONV_W - 1) * stride
    halo = max(SUBLANES, keep)
    assert tm >= keep
    row = pl.BlockSpec((None, tm, D_MODEL), lambda g, s: (g, s, 0))
    st = pl.BlockSpec((None, keep, D_FF), lambda g, s: (g, 0, 0))
    out_norm = out_gain is not None
    extra_specs, extra_args = ([_const_spec((1, D_MODEL))], [out_gain]) if out_norm else ([], [])
    return pl.pallas_call(
        functools.partial(_ffn_kernel, tm=tm, stride=stride, halo=halo, out_norm=out_norm),
        grid=(G, T // tm),
        in_specs=[row, _const_spec((1, D_MODEL)), _layer_spec((D_MODEL, 2 * D_FF), li),
                  _const_spec((CONV_W, D_FF)), _const_spec((1, D_FF)), _layer_spec((D_FF, D_MODEL), li), st]
        + extra_specs,
        out_specs=[row, st],
        out_shape=[jax.ShapeDtypeStruct((G, T, D_MODEL), F32), jax.ShapeDtypeStruct((G, keep, D_FF), F32)],
        scratch_shapes=[pltpu.VMEM((halo + tm, D_FF), F32)],
        compiler_params=_cparams(("arbitrary", "arbitrary")), name="ffn",
    )(x, gn, wup, cw, cb, wdn, cprev, *extra_args)


def _to_wkv_sample(a, Bn):
    T = a.shape[0] // Bn
    a = a.reshape(T, Bn, B_HEADS, B_HD).transpose(0, 3, 1, 2).reshape(T, B_HD, Bn * B_HEADS // LANES, LANES)
    return a.transpose(2, 0, 1, 3)


def _o_from_wkv_sample(o, Bn):
    LG, T = o.shape[0], o.shape[1]
    o = o.transpose(1, 2, 0, 3).reshape(T, B_HD, Bn, B_HEADS).transpose(0, 2, 3, 1)
    return o.reshape(T * Bn, B_WIDTH)


def _state_to_wkv_sample(st):
    Bn = st.shape[0]
    st = st.transpose(2, 3, 0, 1).reshape(B_HD, B_HD, Bn * B_HEADS // LANES, LANES)
    return st.transpose(2, 0, 1, 3)


def _state_from_wkv_sample(st, Bn):
    st = st.transpose(1, 2, 0, 3).reshape(B_HD, B_HD, Bn, B_HEADS)
    return st.transpose(2, 3, 0, 1)


def _layer_params(W, e):
    z = jnp.zeros((DECAY_RANK, B_WIDTH), F32)
    head = jnp.arange(B_WIDTH) // B_HD
    same = (head[:, None] == head[None, :])
    return {
        "mu": W["rwkv_mu"][e][None], "w0": W["rwkv_w0"][e][None],
        "w2p": jnp.concatenate([W["rwkv_w2"][e], z], axis=0).astype(BF16),
        "a0": W["rwkv_a0"][e][None],
        "a2p": jnp.concatenate([z, W["rwkv_a2"][e]], axis=0).astype(BF16),
        "g2": W["rwkv_g2"][e].astype(BF16),
        "k_k": W["rwkv_k_k"][e][None], "k_a": W["rwkv_k_a"][e][None],
        "r_k": W["rwkv_r_k"][e].reshape(1, B_WIDTH),
        "seg": same.astype(BF16), "avg": (same.astype(F32) / B_HD).astype(BF16),
    }


def _trunk(x, W, *, prompt, wkv_in, shift_in, pool_in, conv_in, attn_ctx):
    G, T, _ = x.shape
    if prompt:
        Bn, stride, pos0 = G, 1, 0
        tm_in = tm_ffn = tm_pool = min(T, ROW_TILE)
    else:
        Bn, stride, pos0 = attn_ctx["batch"], attn_ctx["batch"], attn_ctx["past_len"]
        tm_in = tm_ffn = tm_pool = T
    N = G * T
    tm_flat = min(N, ROW_TILE)
    k_rows, v_rows, wkv_out, shift_out, pool_out, conv_out = [], [], [], [], [], []
    for l in range(DEPTH):
        gn = W["norm_mix"][l][None]
        if l % 2 == 0:
            e = l // 2
            lam_init = 0.8 - 0.6 * math.exp(-0.3 * l)
            prm = _layer_params(W, e)
            (k, v, qkv, s_new, nkk, d, b, kt, r, vb, g, bonus) = _inproj(
                x, gn, W["w_in_bf"], e, shift_in[e], prm, tm=tm_in, stride=stride)
            lq = W["diff_lambda"][e]
            sub = W["diff_subln"][e][None]
            if prompt:
                oa = _pattn(qkv, lq, sub, lam_init, tq=min(T, ATTN_TILE)).reshape(N, A_WIDTH)
                o, st = _wkv_chunked(nkk, d, b, kt, r, vb)
                o = o.reshape(N, B_WIDTH)
            else:
                nt = T // Bn
                bh = lambda a: a.reshape(nt, Bn, A_HEADS, 2 * A_HD).transpose(1, 2, 0, 3)
                padr = lambda a, rows: jnp.pad(a, ((0, 0), (0, 0), (0, rows - a.shape[2]), (0, 0)))
                q8 = padr(bh(qkv[0, :, :A_WIDTH]), SROWS // 2)
                comp = (jnp.arange(2 * A_HD) // A_HD)
                qh = jnp.concatenate([jnp.where(comp == c, q8, jnp.zeros((), BF16)) for c in range(2)], axis=2)
                oa = _sattn(attn_ctx["page_table"], qh, padr(bh(k[0]), NEW_KEY_ROWS), padr(bh(v[0]), NEW_KEY_ROWS),
                            lq, sub, attn_ctx["cache_k"], attn_ctx["cache_v"], e, nt, lam_init)
                oa = oa[:, :nt].transpose(1, 0, 2).reshape(N, A_WIDTH).astype(BF16)
                streams = [_to_wkv_sample(a[0], Bn) for a in (nkk, d, b, kt, r, vb)]
                o, st = _wkv(*streams, _state_to_wkv_sample(wkv_in[e]), tb=nt)
                o = _o_from_wkv_sample(o, Bn)
                st = _state_from_wkv_sample(st, Bn)
            x = _mixout(x.reshape(N, D_MODEL), oa, o, g.reshape(N, B_WIDTH), bonus.reshape(N, B_WIDTH),
                        W["rwkv_gn_w"][e][None], W["rwkv_gn_b"][e][None], prm["avg"], W["w_out_bf"], e,
                        tm=tm_flat).reshape(G, T, D_MODEL)
            k_rows.append(k)
            v_rows.append(v)
            wkv_out.append(st)
            shift_out.append(s_new)
        else:
            o_ = l // 2
            x, buf = _pool(x, gn, W["pool_w_bf"][o_], W["pool_scale"][o_][None], pool_in[o_],
                           tm=tm_pool, stride=stride, pos0=pos0)
            pool_out.append(buf)
        x, cbuf = _ffn(x, W["norm_ffn"][l][None], W["ffn_up_bf"], W["ffn_conv_w"][l],
                       W["ffn_conv_b"][l][None], W["ffn_down_bf"], l, conv_in[l], tm=tm_ffn, stride=stride,
                       out_gain=W["norm_final"][None] if l == DEPTH - 1 else None)
        conv_out.append(cbuf)
    return (x, jnp.stack(k_rows), jnp.stack(v_rows), jnp.stack(wkv_out), jnp.stack(shift_out),
            jnp.stack(pool_out), jnp.stack(conv_out))


def kernel(x_prompt, x_sample, cache_k, cache_v, state_wkv, state_shift, state_pool, state_ffn_conv,
           page_table, norm_mix, norm_ffn, norm_final, w_in, w_out, diff_lambda, diff_subln,
           rwkv_mu, rwkv_w0, rwkv_w2, rwkv_a0, rwkv_a2, rwkv_g2, rwkv_k_k, rwkv_k_a, rwkv_r_k,
           rwkv_gn_w, rwkv_gn_b, pool_w, pool_scale, ffn_up, ffn_conv_w, ffn_conv_b, ffn_down):
    W = dict(norm_mix=norm_mix, norm_ffn=norm_ffn, norm_final=norm_final,
             diff_lambda=diff_lambda, diff_subln=diff_subln, rwkv_mu=rwkv_mu, rwkv_w0=rwkv_w0,
             rwkv_w2=rwkv_w2, rwkv_a0=rwkv_a0, rwkv_a2=rwkv_a2, rwkv_g2=rwkv_g2, rwkv_k_k=rwkv_k_k,
             rwkv_k_a=rwkv_k_a, rwkv_r_k=rwkv_r_k, rwkv_gn_w=rwkv_gn_w, rwkv_gn_b=rwkv_gn_b,
             pool_scale=pool_scale, ffn_conv_w=ffn_conv_w, ffn_conv_b=ffn_conv_b,
             w_in_bf=w_in.astype(BF16), w_out_bf=w_out.astype(BF16), pool_w_bf=pool_w.astype(BF16),
             ffn_up_bf=ffn_up.astype(BF16), ffn_down_bf=ffn_down.astype(BF16))
    n_even, n_odd = state_wkv.shape[0], state_pool.shape[0]
    Bp, Sp, _ = x_prompt.shape
    Bs, Ts, _ = x_sample.shape
    n_pages = page_table.shape[1]
    past_len = n_pages * cache_k.shape[2]

    y_p, k_p, v_p, wkv_p, shift_p, pool_p, conv_p = _trunk(
        x_prompt, W, prompt=True, wkv_in=None,
        shift_in=jnp.zeros((n_even, Bp, 1, B_COLS), F32),
        pool_in=jnp.zeros((n_odd, Bp, POOL_BUF, D_MODEL), F32),
        conv_in=jnp.zeros((DEPTH, Bp, CONV_W - 1, D_FF), F32), attn_ctx=None)
    k_p = k_p.reshape(n_even, Bp, Sp, A_HEADS, 2 * A_HD)
    v_p = v_p.reshape(n_even, Bp, Sp, A_HEADS, 2 * A_HD)
    shift_p = shift_p.reshape(n_even, Bp, B_COLS)

    tmaj = lambda a: jnp.swapaxes(a, -3, -2)
    flat = lambda a: a.reshape(a.shape[:-3] + (1, a.shape[-3] * a.shape[-2], a.shape[-1]))
    ctx = dict(batch=Bs, past_len=past_len, page_table=page_table,
               cache_k=cache_k.reshape(cache_k.shape[:2] + (PAGE_SIZE * A_HEADS, 2 * A_HD)),
               cache_v=cache_v.reshape(cache_v.shape[:2] + (PAGE_SIZE * A_HEADS, 2 * A_HD)))
    y_s, k_s, v_s, wkv_s, shift_s, pool_s, conv_s = _trunk(
        flat(tmaj(x_sample)), W, prompt=False, wkv_in=state_wkv,
        shift_in=state_shift[:, None], pool_in=flat(tmaj(state_pool)),
        conv_in=flat(tmaj(state_ffn_conv)), attn_ctx=ctx)
    unflat = lambda a, j: tmaj(a.reshape(a.shape[:-3] + (j, Bs, a.shape[-1])))
    y_s = unflat(y_s, Ts)
    k_s = k_s.reshape(n_even, Ts, Bs, A_HEADS, 2 * A_HD).transpose(0, 2, 1, 3, 4)
    v_s = v_s.reshape(n_even, Ts, Bs, A_HEADS, 2 * A_HD).transpose(0, 2, 1, 3, 4)
    shift_s = shift_s.reshape(n_even, Bs, B_COLS)
    pool_s = unflat(pool_s, POOL_BUF)
    conv_s = unflat(conv_s, CONV_W - 1)
    return (y_p, y_s, k_p, v_p, wkv_p, shift_p, pool_p, conv_p,
            k_s, v_s, wkv_s, shift_s, pool_s, conv_s)
```

```python
import functools
import math

import jax
import jax.numpy as jnp
from jax import lax
from jax.experimental import pallas as pl
from jax.experimental.pallas import tpu as pltpu

F32 = jnp.float32
BF16 = jnp.bfloat16

D_MODEL = 1024
DEPTH = 4
PAGE_SIZE = 128
A_HD = 64
A_HEADS = 4
A_WIDTH = 512
A_COLS = 3 * A_WIDTH
A_SCALE = A_HD ** -0.5
Q_SCALE = A_SCALE * math.log2(math.e)
B_HD = 64
B_WIDTH = 512
B_HEADS = 8
DECAY_RANK = 64
ICLR_RANK = 64
GATE_RANK = 128
B_COLS = 3 * B_WIDTH + DECAY_RANK + ICLR_RANK + GATE_RANK
IN_COLS = A_COLS + B_COLS
POOL_WINDOWS = (2, 4, 8, 16)
POOL_GW = D_MODEL // len(POOL_WINDOWS)
POOL_BUF = max(POOL_WINDOWS) - 1
D_FF = 2816
CONV_W = 3
NORM_EPS = 1e-6
SUBLN_EPS = 1e-5
GN_EPS = 64e-5

LANES = 128
SUBLANES = 8
VMEM_LIMIT = 56 * 1024 * 1024
ROW_TILE = 512
LIGHT_ROW_TILE = 1024
ATTN_TILE = 1024
FF_CHUNK = 2816
PAGES_PER_STEP = 32
NEW_KEY_ROWS = 16
SROWS = 16
HEAD_LOG2 = 6


def _div_nonneg(x, n):
    if n == 1:
        return x
    if n & (n - 1) == 0:
        return lax.shift_right_logical(x, n.bit_length() - 1)
    return x // n


def _cparams(sem):
    return pltpu.CompilerParams(dimension_semantics=sem, vmem_limit_bytes=VMEM_LIMIT)


def _const_spec(shape):
    nd = len(shape)
    return pl.BlockSpec(shape, lambda *_: (0,) * nd, pipeline_mode=pl.Buffered(1))


def _layer_spec(shape, li):
    nd = len(shape)
    return pl.BlockSpec((None,) + tuple(shape), lambda *_: (li,) + (0,) * nd, pipeline_mode=pl.Buffered(1))


def _rms(x, g, eps):
    return x * lax.rsqrt(jnp.mean(x * x, axis=-1, keepdims=True) + eps) * g


def _bf16_dot(x, m):
    return jnp.dot(x.astype(BF16), m, preferred_element_type=F32)


def _lane_tile(x, n):
    return jnp.concatenate([x] * n, axis=1)


def _softplus(z):
    return jnp.maximum(z, 0.0) + jnp.log1p(jnp.exp(-jnp.abs(z)))


def _inproj_kernel(x_ref, gn_ref, win_ref, sprev_ref, mu_ref, w0_ref, w2p_ref, a0_ref, a2p_ref,
                   g2_ref, kk_ref, ka_ref, rk_ref, seg_ref,
                   k_ref, v_ref, qkv_ref, sout_ref, nkk_ref, d_ref, b_ref, kt_ref, r_ref,
                   vb_ref, g_ref, bonus_ref, ext_ref, *, tm, stride, halo):
    s = pl.program_id(1)
    ns = pl.num_programs(1)
    h = _rms(x_ref[...], gn_ref[...], NORM_EPS).astype(BF16)
    p = jnp.dot(h, win_ref[...], preferred_element_type=F32)
    for hh in range(A_HEADS):
        c0 = A_WIDTH + hh * 2 * A_HD
        k_ref[pl.ds(hh, tm, stride=A_HEADS), :] = p[:, c0:c0 + 2 * A_HD]
        v_ref[pl.ds(hh, tm, stride=A_HEADS), :] = p[:, A_WIDTH + c0:A_WIDTH + c0 + 2 * A_HD]
    qkv_ref[:, :A_WIDTH] = (p[:, :A_WIDTH] * Q_SCALE).astype(BF16)
    qkv_ref[:, A_WIDTH:] = p[:, A_WIDTH:A_COLS].astype(BF16)
    pb = p[:, A_COLS:]

    @pl.when(s == 0)
    def _():
        ext_ref[halo - stride:halo, :] = sprev_ref[...]

    ext_ref[halo:halo + tm, :] = pb
    prev = ext_ref[halo - stride:halo - stride + tm, :]
    ext_ref[halo - stride:halo, :] = ext_ref[halo + tm - stride:halo + tm, :]

    @pl.when(s == ns - 1)
    def _():
        sout_ref[...] = ext_ref[halo + tm - stride:halo + tm, :]

    xm = pb + (prev - pb) * mu_ref[...]
    r = xm[:, :B_WIDTH]
    k = xm[:, B_WIDTH:2 * B_WIDTH]
    v = xm[:, 2 * B_WIDTH:3 * B_WIDTH]
    wa = xm[:, 3 * B_WIDTH:3 * B_WIDTH + DECAY_RANK + ICLR_RANK]
    gd = xm[:, 3 * B_WIDTH + DECAY_RANK + ICLR_RANK:]
    wlin = w0_ref[...] + jnp.dot(jnp.tanh(wa).astype(BF16), w2p_ref[...], preferred_element_type=F32)
    w = -_softplus(-wlin) - 0.5
    log_decay = -jnp.exp(w)
    a = jax.nn.sigmoid(a0_ref[...] + jnp.dot(wa.astype(BF16), a2p_ref[...], preferred_element_type=F32))
    g = jnp.dot(jax.nn.sigmoid(gd).astype(BF16), g2_ref[...], preferred_element_type=F32)
    kk = k * kk_ref[...]
    ss = _bf16_dot(kk * kk, seg_ref[...])
    kk = kk / jnp.maximum(jnp.sqrt(ss), 1e-12)
    kt = k * (1.0 + (a - 1.0) * ka_ref[...])
    bonus = _bf16_dot(r * kt * rk_ref[...], seg_ref[...]) * v
    nkk_ref[...] = -kk
    d_ref[...] = log_decay
    b_ref[...] = kk * a
    kt_ref[...] = kt
    r_ref[...] = r
    vb_ref[...] = v
    g_ref[...] = g
    bonus_ref[...] = bonus


def _inproj(x, gn, win, li, sprev, prm, *, tm, stride):
    G, T, _ = x.shape
    halo = max(SUBLANES, stride)
    ns = T // tm
    row = lambda w: pl.BlockSpec((None, tm, w), lambda g, s: (g, s, 0))
    stream = jax.ShapeDtypeStruct((G, T, B_WIDTH), F32)
    in_specs = [row(D_MODEL), _const_spec((1, D_MODEL)), _layer_spec((D_MODEL, IN_COLS), li),
                pl.BlockSpec((None, stride, B_COLS), lambda g, s: (g, 0, 0)),
                _const_spec((1, B_COLS)), _const_spec((1, B_WIDTH)), _const_spec((LANES, B_WIDTH)),
                _const_spec((1, B_WIDTH)), _const_spec((LANES, B_WIDTH)), _const_spec((GATE_RANK, B_WIDTH)),
                _const_spec((1, B_WIDTH)), _const_spec((1, B_WIDTH)), _const_spec((1, B_WIDTH)),
                _const_spec((B_WIDTH, B_WIDTH))]
    kv_shape = jax.ShapeDtypeStruct((G, T * A_HEADS, 2 * A_HD), F32)
    kv_spec = pl.BlockSpec((None, tm * A_HEADS, 2 * A_HD), lambda g, s: (g, s, 0))
    out_shape = [kv_shape, kv_shape, jax.ShapeDtypeStruct((G, T, A_COLS), BF16),
                 jax.ShapeDtypeStruct((G, stride, B_COLS), F32)] + [stream] * 8
    out_specs = [kv_spec, kv_spec, row(A_COLS),
                 pl.BlockSpec((None, stride, B_COLS), lambda g, s: (g, 0, 0))] + [row(B_WIDTH)] * 8
    return pl.pallas_call(
        functools.partial(_inproj_kernel, tm=tm, stride=stride, halo=halo),
        grid=(G, ns), in_specs=in_specs, out_specs=out_specs, out_shape=out_shape,
        scratch_shapes=[pltpu.VMEM((halo + tm, B_COLS), F32)],
        compiler_params=_cparams(("arbitrary", "arbitrary")), name="inproj",
    )(x, gn, win, sprev, prm["mu"], prm["w0"], prm["w2p"], prm["a0"], prm["a2p"], prm["g2"],
      prm["k_k"], prm["k_a"], prm["r_k"], prm["seg"])


def _lambda_full(lq, lam_init):
    s01 = jnp.sum(lq[0:1, :] * lq[1:2, :], axis=-1, keepdims=True)
    s23 = jnp.sum(lq[2:3, :] * lq[3:4, :], axis=-1, keepdims=True)
    return jnp.exp(s01) - jnp.exp(s23) + lam_init


def _pattn_kernel(qi_ref, kj_ref, q_ref, k_ref, v_ref, lq_ref, sub_ref, o_ref,
                  m0_ref, l0_ref, a0_ref, m1_ref, l1_ref, a1_ref, *, tq, lam_init):
    qi = qi_ref[pl.program_id(2)]
    kj = kj_ref[pl.program_id(2)]

    @pl.when(kj == 0)
    def _():
        for m_ref, l_ref, a_ref in ((m0_ref, l0_ref, a0_ref), (m1_ref, l1_ref, a1_ref)):
            m_ref[...] = jnp.full(m_ref.shape, -jnp.inf, F32)
            l_ref[...] = jnp.zeros(l_ref.shape, F32)
            a_ref[...] = jnp.zeros(a_ref.shape, F32)

    def accumulate(diagonal):
        q = q_ref[...]
        k = k_ref[...]
        v = v_ref[...]
        lane = lax.broadcasted_iota(jnp.int32, q.shape, 1)
        zero = jnp.zeros_like(q)
        if diagonal:
            keep = (lax.broadcasted_iota(jnp.int32, (tq, tq), 1)
                    <= lax.broadcasted_iota(jnp.int32, (tq, tq), 0))
        refs = ((m0_ref, l0_ref, a0_ref), (m1_ref, l1_ref, a1_ref))
        comps = range(2)
        qc = [jnp.where((lane >= c * A_HD) & (lane < (c + 1) * A_HD), q, zero) for c in comps]
        s = [lax.dot_general(qc[c], k, (((1,), (1,)), ((), ())), preferred_element_type=F32) for c in comps]
        if diagonal:
            s = [jnp.where(keep, s[c], -jnp.inf) for c in comps]
        m_old = [refs[c][0][...] for c in comps]
        m_new = [jnp.maximum(m_old[c], jnp.max(s[c], axis=-1, keepdims=True)) for c in comps]
        alpha = [jnp.exp2(m_old[c] - m_new[c]) for c in comps]
        p = [jnp.exp2(s[c] - _lane_tile(m_new[c], tq // LANES)) for c in comps]
        for c in comps:
            m_ref, l_ref, a_ref = refs[c]
            l_ref[...] = alpha[c] * l_ref[...] + jnp.sum(p[c], axis=-1, keepdims=True)
            a_ref[...] = alpha[c] * a_ref[...] + jnp.dot(p[c].astype(BF16), v, preferred_element_type=F32)
            m_ref[...] = m_new[c]

    @pl.when(kj < qi)
    def _():
        accumulate(False)

    @pl.when(kj == qi)
    def _():
        accumulate(True)
        lam = _lambda_full(lq_ref[...], lam_init)
        o = a0_ref[...] / l0_ref[...] - lam * (a1_ref[...] / l1_ref[...])
        o = _rms(o, sub_ref[...], SUBLN_EPS) * (1.0 - lam_init)
        o_ref[...] = o.astype(o_ref.dtype)


def _pattn(qkv, lq, sub, lam_init, *, tq):
    B, S, _ = qkv.shape
    nq = S // tq
    nh = A_HEADS
    pairs = [(i, j) for i in range(nq) for j in range(i + 1)]
    qi_tab = jnp.asarray([p[0] for p in pairs], jnp.int32)
    kj_tab = jnp.asarray([p[1] for p in pairs], jnp.int32)
    q_spec = pl.BlockSpec((None, tq, LANES), lambda b, h, n, qt, kt: (b, qt[n], h))
    k_spec = pl.BlockSpec((None, tq, LANES), lambda b, h, n, qt, kt: (b, kt[n], nh + h))
    v_spec = pl.BlockSpec((None, tq, LANES), lambda b, h, n, qt, kt: (b, kt[n], 2 * nh + h))
    grid_spec = pltpu.PrefetchScalarGridSpec(
        num_scalar_prefetch=2, grid=(B, nh, len(pairs)),
        in_specs=[q_spec, k_spec, v_spec,
                  pl.BlockSpec((4, A_HD), lambda b, h, n, qt, kt: (0, 0)),
                  pl.BlockSpec((1, LANES), lambda b, h, n, qt, kt: (0, 0))],
        out_specs=pl.BlockSpec((None, tq, LANES), lambda b, h, n, qt, kt: (b, qt[n], h)),
        scratch_shapes=[pltpu.VMEM((tq, LANES), F32)] * 6)
    return pl.pallas_call(
        functools.partial(_pattn_kernel, tq=tq, lam_init=lam_init),
        grid_spec=grid_spec, out_shape=jax.ShapeDtypeStruct((B, S, A_WIDTH), BF16),
        compiler_params=_cparams(("arbitrary",) * 3), name="pattn",
    )(qi_tab, kj_tab, qkv, qkv, qkv, lq, sub)


def _sattn_kernel(pt_ref, q_ref, kn_ref, vn_ref, lq_ref, sub_ref, *rest, npg, nt, lam_init):
    k_refs = rest[:npg]
    v_refs = rest[npg:2 * npg]
    o_ref = rest[2 * npg]
    m_ref, l_ref, acc_ref = rest[2 * npg + 1:]
    j = pl.program_id(1)
    nj = pl.num_programs(1)

    @pl.when(j == 0)
    def _():
        m_ref[...] = jnp.full(m_ref.shape, -jnp.inf, F32)
        l_ref[...] = jnp.zeros(l_ref.shape, F32)
        acc_ref[...] = jnp.zeros(acc_ref.shape, F32)

    nt_dims = (((1,), (1,)), ((), ()))
    qs = [q_ref[h] for h in range(A_HEADS)]

    def head_rows(pg_ref, h):
        return pg_ref[pl.ds(h, PAGE_SIZE, stride=A_HEADS), :].astype(BF16)

    def update(s, pv):
        m_old = m_ref[...]
        m_new = jnp.maximum(m_old, jnp.max(s, axis=-1, keepdims=True))
        alpha = jnp.exp2(m_old - m_new)
        p = jnp.exp2(s - (_lane_tile(m_new, s.shape[1] // LANES) if s.shape[1] > LANES
                          else m_new[:, :s.shape[1]]))
        l_ref[...] = alpha * l_ref[...] + jnp.sum(p, axis=-1, keepdims=True)
        acc_ref[...] = alpha * acc_ref[...] + pv(p.astype(BF16))
        m_ref[...] = m_new

    s_heads = [jnp.concatenate(
        [lax.dot_general(qs[h], head_rows(kr, h), nt_dims, preferred_element_type=F32) for kr in k_refs],
        axis=1) for h in range(A_HEADS)]

    def pv_pages(p):
        outs = []
        for h in range(A_HEADS):
            ph = p[h * SROWS:(h + 1) * SROWS, :]
            tot = None
            for i, vr in enumerate(v_refs):
                part = jnp.dot(ph[:, i * PAGE_SIZE:(i + 1) * PAGE_SIZE], head_rows(vr, h),
                               preferred_element_type=F32)
                tot = part if tot is None else tot + part
            outs.append(tot)
        return jnp.concatenate(outs, axis=0)

    update(jnp.concatenate(s_heads, axis=0), pv_pages)

    @pl.when(j == nj - 1)
    def _():
        s_new = jnp.concatenate(
            [lax.dot_general(qs[h], kn_ref[h].astype(BF16), nt_dims, preferred_element_type=F32)
             for h in range(A_HEADS)], axis=0)
        key = lax.broadcasted_iota(jnp.int32, s_new.shape, 1)
        tok = lax.broadcasted_iota(jnp.int32, s_new.shape, 0) & (SROWS // 2 - 1)
        s_new = jnp.where((key <= tok) & (key < nt), s_new, -jnp.inf)

        def pv_new(p):
            return jnp.concatenate(
                [jnp.dot(p[h * SROWS:(h + 1) * SROWS, :], vn_ref[h].astype(BF16), preferred_element_type=F32)
                 for h in range(A_HEADS)], axis=0)

        update(s_new, pv_new)
        lam = _lambda_full(lq_ref[...], lam_init)
        half = SROWS // 2
        outs = []
        for h in range(A_HEADS):
            r0 = h * SROWS
            o = (acc_ref[r0:r0 + half, :] / l_ref[r0:r0 + half, :]
                 - lam * (acc_ref[r0 + half:r0 + SROWS, :] / l_ref[r0 + half:r0 + SROWS, :]))
            outs.append(_rms(o, sub_ref[...], SUBLN_EPS) * (1.0 - lam_init))
        o_ref[...] = jnp.concatenate(outs, axis=-1)


def _sattn(page_table, qh, k_new, v_new, lq, sub, kc, vc, e, nt, lam_init):
    Bn = qh.shape[0]
    n_pages = page_table.shape[1]
    npg = PAGES_PER_STEP
    nj = n_pages // npg
    page_rows = PAGE_SIZE * A_HEADS
    per_b = lambda r: pl.BlockSpec((None, A_HEADS, r, LANES), lambda b, j, pt: (b, 0, 0, 0))

    def page_spec(i):
        return pl.BlockSpec((None, None, page_rows, LANES), lambda b, j, pt: (e, pt[b, j * npg + i], 0, 0))

    in_specs = ([per_b(SROWS), per_b(NEW_KEY_ROWS), per_b(NEW_KEY_ROWS),
                 pl.BlockSpec((4, A_HD), lambda b, j, pt: (0, 0)),
                 pl.BlockSpec((1, LANES), lambda b, j, pt: (0, 0))]
                + [page_spec(i) for i in range(npg)] * 2)
    nr = A_HEADS * SROWS
    grid_spec = pltpu.PrefetchScalarGridSpec(
        num_scalar_prefetch=1, grid=(Bn, nj), in_specs=in_specs,
        out_specs=pl.BlockSpec((None, SROWS // 2, A_WIDTH), lambda b, j, pt: (b, 0, 0)),
        scratch_shapes=[pltpu.VMEM((nr, LANES), F32)] * 3)
    return pl.pallas_call(
        functools.partial(_sattn_kernel, npg=npg, nt=nt, lam_init=lam_init),
        grid_spec=grid_spec, out_shape=jax.ShapeDtypeStruct((Bn, SROWS // 2, A_WIDTH), F32),
        compiler_params=_cparams(("arbitrary", "arbitrary")), name="sattn",
    )(page_table, qh, k_new, v_new, lq, sub, *([kc] * npg), *([vc] * npg))


def _wkv_kernel(nkk_ref, d_ref, b_ref, kt_ref, r_ref, v_ref, s0_ref, o_ref, st_ref, s_scr, *, tb, nv):
    t0 = pl.program_id(1)

    @pl.when(t0 == 0)
    def _():
        s_scr[...] = s0_ref[...]

    def step(t, carry):
        nkk = nkk_ref[t]
        d = jnp.exp(d_ref[t])
        b = b_ref[t]
        kt = kt_ref[t]
        r = r_ref[t]
        dr = d * r
        br = jnp.sum(b * r, axis=0, keepdims=True)
        kr = jnp.sum(kt * r, axis=0, keepdims=True)

        def row(vr, c):
            S = s_scr[vr]
            sa = jnp.sum(S * nkk, axis=0, keepdims=True)
            qo = jnp.sum(S * dr, axis=0, keepdims=True)
            vv = v_ref[t, pl.ds(vr, 1), :]
            s_scr[vr] = S * d + sa * b + vv * kt
            o_ref[t, pl.ds(vr, 1), :] = qo + sa * br + vv * kr
            return c

        return lax.fori_loop(0, nv, row, carry, unroll=2)

    lax.fori_loop(0, tb, step, 0)
    st_ref[...] = s_scr[...]


def _wkv(nkk, d, b, kt, r, v, s0, *, tb):
    LG, T, _, _ = nkk.shape
    nv = v.shape[2]
    kspec = pl.BlockSpec((None, tb, B_HD, LANES), lambda g, t: (g, t, 0, 0))
    vspec = pl.BlockSpec((None, tb, nv, LANES), lambda g, t: (g, t, 0, 0))
    sspec = pl.BlockSpec((None, nv, B_HD, LANES), lambda g, t: (g, 0, 0, 0))
    return pl.pallas_call(
        functools.partial(_wkv_kernel, tb=tb, nv=nv),
        grid=(LG, T // tb), in_specs=[kspec] * 5 + [vspec, sspec],
        out_specs=[vspec, sspec],
        out_shape=[jax.ShapeDtypeStruct((LG, T, nv, LANES), F32),
                   jax.ShapeDtypeStruct((LG, nv, B_HD, LANES), F32)],
        scratch_shapes=[pltpu.VMEM((nv, B_HD, LANES), F32)],
        compiler_params=_cparams(("arbitrary", "arbitrary")), name="wkv",
    )(nkk, d, b, kt, r, v, s0)


WKV_CHUNK = 64
WKV_GROUP = 4


def _split3_dot(m, x):
    x1 = x.astype(BF16)
    r1 = x - x1.astype(F32)
    x2 = r1.astype(BF16)
    x3 = (r1 - x2.astype(F32)).astype(BF16)
    return (jnp.dot(m, x1, preferred_element_type=F32) + jnp.dot(m, x2, preferred_element_type=F32)
            + jnp.dot(m, x3, preferred_element_type=F32))


def _wkv_chunk_math(a, ld, b, kt, r, v, s):
    L, HW = WKV_CHUNK, WKV_GROUP * B_HD
    nt_dims = (((1,), (1,)), ((), ()))
    tn_dims = (((0,), (0,)), ((), ()))
    row = lax.broadcasted_iota(jnp.int32, (L, HW), 0)
    col = lax.broadcasted_iota(jnp.int32, (L, HW), 1) & (B_HD - 1)
    strict = col < row
    incl = col <= row
    blk = lax.shift_right_logical
    bd_mask = (lax.shift_right_logical(lax.broadcasted_iota(jnp.int32, (HW, HW), 0), HEAD_LOG2)
               == lax.shift_right_logical(lax.broadcasted_iota(jnp.int32, (HW, HW), 1), HEAD_LOG2))

    def bd(y):
        y = y.astype(BF16)
        return jnp.where(bd_mask, jnp.concatenate([y] * WKV_GROUP, axis=0), jnp.zeros((), BF16))

    def hprod(x, y):
        return jnp.dot(x.astype(BF16), bd(y), preferred_element_type=F32)

    def each(f, *lists):
        return [f(*xs) for xs in zip(*lists)]

    def dotg(dims):
        return lambda x, y: lax.dot_general(x, y, dims, preferred_element_type=F32)

    tri = (lax.broadcasted_iota(jnp.int32, (L, L), 1) <= lax.broadcasted_iota(jnp.int32, (L, L), 0)).astype(BF16)
    logp = each(lambda x: _split3_dot(tri, x), ld)
    ep = each(jnp.exp, logp)
    p_last = each(lambda x: x[L - 1:L, :], ep)
    en = each(lambda x: jnp.exp(-x), logp)
    at = each(lambda x, lp, l_: x * jnp.exp(lp - l_), a, logp, ld)
    rt = each(jnp.multiply, r, ep)
    bt = each(jnp.multiply, b, en)
    ktt = each(jnp.multiply, kt, en)
    bh = each(lambda x, pp: (x * pp).astype(BF16), bt, p_last)
    kh = each(lambda x, pp: (x * pp).astype(BF16), ktt, p_last)

    lhs = each(lambda x, y: jnp.concatenate([x, y], axis=0).astype(BF16), at, rt)
    xb = each(lambda x, y: dotg(nt_dims)(x, bd(y)), lhs, bt)
    xk = each(lambda x, y: dotg(nt_dims)(x, bd(y)), lhs, ktt)
    n = each(lambda x: jnp.where(strict, x[:L], 0.0), xb)
    arb = each(lambda x: jnp.where(incl, x[L:], 0.0), xb)
    aak = each(lambda x: jnp.where(strict, x[:L], 0.0), xk)
    ark = each(lambda x: jnp.where(incl, x[L:], 0.0), xk)

    eye = jnp.where(col == row, 1.0, 0.0)
    tm = each(lambda x: eye + jnp.where(blk(row, 1) == blk(col, 1), x, 0.0), n)
    for lg in range(1, HEAD_LOG2):
        sel = (blk(row, lg + 1) == blk(col, lg + 1)) & (blk(row, lg) != blk(col, lg))
        w = each(lambda t_, x: hprod(t_, jnp.where(sel, x, 0.0)), tm, n)
        tm = each(lambda t_, w_: t_ + hprod(w_, t_), tm, w)

    rloc = each(hprod, aak, v)
    uloc = each(hprod, tm, rloc)
    ah = each(hprod, tm, at)
    ro = each(lambda x, y, z: (x + hprod(y, z)).astype(BF16), rt, arb, ah)
    oc = each(lambda x, u, y, vv: hprod(x, u) + hprod(y, vv), arb, uloc, ark, v)
    mpp = each(lambda x, y: jnp.where(bd_mask, dotg(tn_dims)(x.astype(BF16), y), 0.0).astype(BF16), ah, bh)
    sc = each(lambda u, y, vv, z: jnp.where(
        bd_mask, dotg(tn_dims)(u.astype(BF16), y) + dotg(tn_dims)(vv.astype(BF16), z), 0.0), uloc, bh, v, kh)

    s_hi = each(lambda x: x.astype(BF16), s)
    s_lo = each(lambda x, h: (x - h.astype(F32)).astype(BF16), s, s_hi)
    o = each(lambda q, h, c_: dotg(nt_dims)(q, h) + c_, ro, s_hi, oc)
    s_new = each(lambda x, pp, h, l_, m_, c_: (x * pp + jnp.dot(h, m_, preferred_element_type=F32)
                                               + jnp.dot(l_, m_, preferred_element_type=F32) + c_),
                 s, p_last, s_hi, s_lo, mpp, sc)
    return o, s_new


def _wkv_chunk_kernel(a_ref, ld_ref, b_ref, kt_ref, r_ref, v_ref, o_ref, st_ref, s_scr, *, nb):
    c = pl.program_id(1)
    nc = pl.num_programs(1)
    HW = WKV_GROUP * B_HD

    @pl.when(c == 0)
    def _():
        s_scr[...] = jnp.zeros(s_scr.shape, F32)

    groups = [(bb, g) for bb in range(nb) for g in range(B_WIDTH // HW)]
    sls = [(bb, slice(None), slice(g * HW, (g + 1) * HW)) for bb, g in groups]
    o, s_new = _wkv_chunk_math(*[[ref[sl] for sl in sls] for ref in (a_ref, ld_ref, b_ref, kt_ref, r_ref, v_ref)],
                               [s_scr[bb, g] for bb, g in groups])
    for sl, (bb, g), o_i, s_i in zip(sls, groups, o, s_new):
        o_ref[sl] = o_i
        s_scr[bb, g] = s_i

    @pl.when(c == nc - 1)
    def _():
        st_ref[...] = s_scr[...]


def _wkv_chunked(a, ld, b, kt, r, v, *, nb=4):
    Bn, T, _ = a.shape
    L, HW = WKV_CHUNK, WKV_GROUP * B_HD
    ng = B_WIDTH // HW
    nb = min(nb, Bn)
    assert Bn % nb == 0 and T % L == 0
    spec = pl.BlockSpec((nb, L, B_WIDTH), lambda bb, c: (bb, c, 0))
    st_spec = pl.BlockSpec((nb, ng, HW, HW), lambda bb, c: (bb, 0, 0, 0))
    o, st = pl.pallas_call(
        functools.partial(_wkv_chunk_kernel, nb=nb),
        grid=(Bn // nb, T // L), in_specs=[spec] * 6, out_specs=[spec, st_spec],
        out_shape=[jax.ShapeDtypeStruct((Bn, T, B_WIDTH), F32), jax.ShapeDtypeStruct((Bn, ng, HW, HW), F32)],
        scratch_shapes=[pltpu.VMEM((nb, ng, HW, HW), F32)],
        compiler_params=_cparams(("arbitrary",) * 2), name="wkv_chunked",
    )(a, ld, b, kt, r, v)
    st = st.reshape(Bn, ng, WKV_GROUP, B_HD, WKV_GROUP, B_HD)
    st = jnp.stack([st[:, :, i, :, i, :] for i in range(WKV_GROUP)], axis=2)
    return o, st.reshape(Bn, B_HEADS, B_HD, B_HD)


def _mixout_kernel(x_ref, oa_ref, o_ref, g_ref, bonus_ref, gnw_ref, gnb_ref, avg_ref, wout_ref, xo_ref):
    o = o_ref[...]
    mu = _bf16_dot(o, avg_ref[...])
    dlt = o - mu
    var = _bf16_dot(dlt * dlt, avg_ref[...])
    on = dlt * lax.rsqrt(var + GN_EPS) * gnw_ref[...] + gnb_ref[...]
    ob = ((on + bonus_ref[...]) * g_ref[...]).astype(BF16)
    y = (jnp.dot(oa_ref[...], wout_ref[0:A_WIDTH, :], preferred_element_type=F32)
         + jnp.dot(ob, wout_ref[A_WIDTH:, :], preferred_element_type=F32))
    xo_ref[...] = x_ref[...] + y


def _mixout(x, oa, o, g, bonus, gnw, gnb, avg, wout, li, *, tm):
    N = x.shape[0]
    row = lambda w: pl.BlockSpec((tm, w), lambda i: (i, 0))
    return pl.pallas_call(
        _mixout_kernel, grid=(N // tm,),
        in_specs=[row(D_MODEL), row(A_WIDTH), row(B_WIDTH), row(B_WIDTH), row(B_WIDTH),
                  _const_spec((1, B_WIDTH)), _const_spec((1, B_WIDTH)),
                  _const_spec((B_WIDTH, B_WIDTH)), _layer_spec((D_MODEL, D_MODEL), li)],
        out_specs=row(D_MODEL), out_shape=jax.ShapeDtypeStruct((N, D_MODEL), F32),
        compiler_params=_cparams(("arbitrary",)), name="mixout",
    )(x, oa, o, g, bonus, gnw, gnb, avg, wout)


def _pool_kernel(x_ref, gn_ref, pw_ref, ps_ref, pprev_ref, xo_ref, pout_ref, ext_ref,
                 *, tm, stride, halo, pos0):
    s = pl.program_id(1)
    ns = pl.num_programs(1)
    keep = POOL_BUF * stride
    x = x_ref[...]
    h = _rms(x, gn_ref[...], NORM_EPS)

    @pl.when(s == 0)
    def _():
        ext_ref[halo - keep:halo, :] = pprev_ref[...]

    ext_ref[halo:halo + tm, :] = h
    t = _div_nonneg(s * tm + lax.broadcasted_iota(jnp.int32, (tm, 1), 0), stride)
    ys = []
    for gi, win in enumerate(POOL_WINDOWS):
        c0, c1 = gi * POOL_GW, (gi + 1) * POOL_GW
        cur = ext_ref[halo:halo + tm, c0:c1]
        wsum = cur
        for jj in range(1, win):
            wsum = wsum + ext_ref[halo - jj * stride:halo - jj * stride + tm, c0:c1]
        cnt = jnp.minimum(pos0 + t + 1, win).astype(F32)
        m = wsum / cnt - cur
        ys.append(jnp.dot(m.astype(BF16), pw_ref[gi], preferred_element_type=F32))
    xo_ref[...] = x + jnp.concatenate(ys, axis=-1) * ps_ref[...]

    @pl.when(s == ns - 1)
    def _():
        pout_ref[...] = ext_ref[halo + tm - keep:halo + tm, :]

    if tm >= keep:
        ext_ref[halo - keep:halo, :] = ext_ref[halo + tm - keep:halo + tm, :]


def _pool(x, gn, pw, ps, pprev, *, tm, stride, pos0):
    G, T, _ = x.shape
    halo = 2 * SUBLANES * stride
    keep = POOL_BUF * stride
    assert T // tm == 1 or tm >= keep
    row = pl.BlockSpec((None, tm, D_MODEL), lambda g, s: (g, s, 0))
    st = pl.BlockSpec((None, keep, D_MODEL), lambda g, s: (g, 0, 0))
    return pl.pallas_call(
        functools.partial(_pool_kernel, tm=tm, stride=stride, halo=halo, pos0=pos0),
        grid=(G, T // tm),
        in_specs=[row, _const_spec((1, D_MODEL)), _const_spec((len(POOL_WINDOWS), POOL_GW, POOL_GW)),
                  _const_spec((1, D_MODEL)), st],
        out_specs=[row, st],
        out_shape=[jax.ShapeDtypeStruct((G, T, D_MODEL), F32), jax.ShapeDtypeStruct((G, keep, D_MODEL), F32)],
        scratch_shapes=[pltpu.VMEM((halo + tm, D_MODEL), F32)],
        compiler_params=_cparams(("arbitrary", "arbitrary")), name="pool",
    )(x, gn, pw, ps, pprev)


def _ffn_kernel(x_ref, gn_ref, wup_ref, cw_ref, cb_ref, wdn_ref, cprev_ref, *rest, tm, stride, halo, out_norm):
    gout_ref = rest[0] if out_norm else None
    xo_ref, cout_ref, ext_ref = rest[-3:]
    s = pl.program_id(1)
    ns = pl.num_programs(1)
    keep = (CONV_W - 1) * stride
    x = x_ref[...]
    h = _rms(x, gn_ref[...], NORM_EPS).astype(BF16)

    @pl.when(s == 0)
    def _():
        ext_ref[halo - keep:halo, :] = cprev_ref[...]

    def up(c):
        c0, c1 = c * FF_CHUNK, (c + 1) * FF_CHUNK
        gate = jnp.dot(h, wup_ref[:, c0:c1], preferred_element_type=F32)
        val = jnp.dot(h, wup_ref[:, D_FF + c0:D_FF + c1], preferred_element_type=F32)
        ext_ref[halo:halo + tm, c0:c1] = gate
        return gate, val

    def down(c, gate, val):
        c0, c1 = c * FF_CHUNK, (c + 1) * FF_CHUNK
        cc = cb_ref[:, c0:c1]
        for jj in range(CONV_W - 1):
            back = (CONV_W - 1 - jj) * stride
            cc = cc + ext_ref[halo - back:halo - back + tm, c0:c1] * cw_ref[jj:jj + 1, c0:c1]
        cc = cc + gate * cw_ref[CONV_W - 1:CONV_W, c0:c1]
        act = (0.5 * cc * (1.0 + lax.erf(cc * (2.0 ** -0.5))) * val).astype(BF16)
        return jnp.dot(act, wdn_ref[c0:c1, :], preferred_element_type=F32)

    nchunk = D_FF // FF_CHUNK
    acc = x
    pending = up(0)
    for c in range(nchunk):
        nxt = up(c + 1) if c + 1 < nchunk else None
        acc = acc + down(c, *pending)
        pending = nxt
    xo_ref[...] = _rms(acc, gout_ref[...], NORM_EPS) if out_norm else acc

    @pl.when(s == ns - 1)
    def _():
        cout_ref[...] = ext_ref[halo + tm - keep:halo + tm, :]

    ext_ref[halo - keep:halo, :] = ext_ref[halo + tm - keep:halo + tm, :]


def _ffn(x, gn, wup, cw, cb, wdn, li, cprev, *, tm, stride, out_gain=None):
    G, T, _ = x.shape
    keep = (CONV_W - 1) * stride
    halo = max(SUBLANES, keep)
    assert tm >= keep
    row = pl.BlockSpec((None, tm, D_MODEL), lambda g, s: (g, s, 0))
    st = pl.BlockSpec((None, keep, D_FF), lambda g, s: (g, 0, 0))
    out_norm = out_gain is not None
    extra_specs, extra_args = ([_const_spec((1, D_MODEL))], [out_gain]) if out_norm else ([], [])
    return pl.pallas_call(
        functools.partial(_ffn_kernel, tm=tm, stride=stride, halo=halo, out_norm=out_norm),
        grid=(G, T // tm),
        in_specs=[row, _const_spec((1, D_MODEL)), _layer_spec((D_MODEL, 2 * D_FF), li),
                  _const_spec((CONV_W, D_FF)), _const_spec((1, D_FF)), _layer_spec((D_FF, D_MODEL), li), st]
        + extra_specs,
        out_specs=[row, st],
        out_shape=[jax.ShapeDtypeStruct((G, T, D_MODEL), F32), jax.ShapeDtypeStruct((G, keep, D_FF), F32)],
        scratch_shapes=[pltpu.VMEM((halo + tm, D_FF), F32)],
        compiler_params=_cparams(("arbitrary", "arbitrary")), name="ffn",
    )(x, gn, wup, cw, cb, wdn, cprev, *extra_args)


def _to_wkv_sample(a, Bn):
    T = a.shape[0] // Bn
    a = a.reshape(T, Bn, B_HEADS, B_HD).transpose(0, 3, 1, 2).reshape(T, B_HD, Bn * B_HEADS // LANES, LANES)
    return a.transpose(2, 0, 1, 3)


def _o_from_wkv_sample(o, Bn):
    LG, T = o.shape[0], o.shape[1]
    o = o.transpose(1, 2, 0, 3).reshape(T, B_HD, Bn, B_HEADS).transpose(0, 2, 3, 1)
    return o.reshape(T * Bn, B_WIDTH)


def _state_to_wkv_sample(st):
    Bn = st.shape[0]
    st = st.transpose(2, 3, 0, 1).reshape(B_HD, B_HD, Bn * B_HEADS // LANES, LANES)
    return st.transpose(2, 0, 1, 3)


def _state_from_wkv_sample(st, Bn):
    st = st.transpose(1, 2, 0, 3).reshape(B_HD, B_HD, Bn, B_HEADS)
    return st.transpose(2, 3, 0, 1)


def _layer_params(W, e):
    z = jnp.zeros((DECAY_RANK, B_WIDTH), F32)
    head = jnp.arange(B_WIDTH) // B_HD
    same = (head[:, None] == head[None, :])
    return {
        "mu": W["rwkv_mu"][e][None], "w0": W["rwkv_w0"][e][None],
        "w2p": jnp.concatenate([W["rwkv_w2"][e], z], axis=0).astype(BF16),
        "a0": W["rwkv_a0"][e][None],
        "a2p": jnp.concatenate([z, W["rwkv_a2"][e]], axis=0).astype(BF16),
        "g2": W["rwkv_g2"][e].astype(BF16),
        "k_k": W["rwkv_k_k"][e][None], "k_a": W["rwkv_k_a"][e][None],
        "r_k": W["rwkv_r_k"][e].reshape(1, B_WIDTH),
        "seg": same.astype(BF16), "avg": (same.astype(F32) / B_HD).astype(BF16),
    }


def _trunk(x, W, *, prompt, wkv_in, shift_in, pool_in, conv_in, attn_ctx):
    G, T, _ = x.shape
    if prompt:
        Bn, stride, pos0 = G, 1, 0
        tm_in = tm_ffn = min(T, ROW_TILE)
        tm_pool = min(T, LIGHT_ROW_TILE)
    else:
        Bn, stride, pos0 = attn_ctx["batch"], attn_ctx["batch"], attn_ctx["past_len"]
        tm_in = tm_ffn = tm_pool = T
    N = G * T
    tm_flat = min(N, LIGHT_ROW_TILE)
    k_rows, v_rows, wkv_out, shift_out, pool_out, conv_out = [], [], [], [], [], []
    for l in range(DEPTH):
        gn = W["norm_mix"][l][None]
        if l % 2 == 0:
            e = l // 2
            lam_init = 0.8 - 0.6 * math.exp(-0.3 * l)
            prm = _layer_params(W, e)
            (k, v, qkv, s_new, nkk, d, b, kt, r, vb, g, bonus) = _inproj(
                x, gn, W["w_in_bf"], e, shift_in[e], prm, tm=tm_in, stride=stride)
            lq = W["diff_lambda"][e]
            sub = W["diff_subln"][e][None]
            if prompt:
                oa = _pattn(qkv, lq, sub, lam_init, tq=min(T, ATTN_TILE)).reshape(N, A_WIDTH)
                o, st = _wkv_chunked(nkk, d, b, kt, r, vb)
                o = o.reshape(N, B_WIDTH)
            else:
                nt = T // Bn
                bh = lambda a: a.reshape(nt, Bn, A_HEADS, 2 * A_HD).transpose(1, 2, 0, 3)
                padr = lambda a, rows: jnp.pad(a, ((0, 0), (0, 0), (0, rows - a.shape[2]), (0, 0)))
                q8 = padr(bh(qkv[0, :, :A_WIDTH]), SROWS // 2)
                comp = (jnp.arange(2 * A_HD) // A_HD)
                qh = jnp.concatenate([jnp.where(comp == c, q8, jnp.zeros((), BF16)) for c in range(2)], axis=2)
                oa = _sattn(attn_ctx["page_table"], qh, padr(bh(k[0]), NEW_KEY_ROWS), padr(bh(v[0]), NEW_KEY_ROWS),
                            lq, sub, attn_ctx["cache_k"], attn_ctx["cache_v"], e, nt, lam_init)
                oa = oa[:, :nt].transpose(1, 0, 2).reshape(N, A_WIDTH).astype(BF16)
                streams = [_to_wkv_sample(a[0], Bn) for a in (nkk, d, b, kt, r, vb)]
                o, st = _wkv(*streams, _state_to_wkv_sample(wkv_in[e]), tb=nt)
                o = _o_from_wkv_sample(o, Bn)
                st = _state_from_wkv_sample(st, Bn)
            x = _mixout(x.reshape(N, D_MODEL), oa, o, g.reshape(N, B_WIDTH), bonus.reshape(N, B_WIDTH),
                        W["rwkv_gn_w"][e][None], W["rwkv_gn_b"][e][None], prm["avg"], W["w_out_bf"], e,
                        tm=tm_flat).reshape(G, T, D_MODEL)
            k_rows.append(k)
            v_rows.append(v)
            wkv_out.append(st)
            shift_out.append(s_new)
        else:
            o_ = l // 2
            x, buf = _pool(x, gn, W["pool_w_bf"][o_], W["pool_scale"][o_][None], pool_in[o_],
                           tm=tm_pool, stride=stride, pos0=pos0)
            pool_out.append(buf)
        x, cbuf = _ffn(x, W["norm_ffn"][l][None], W["ffn_up_bf"], W["ffn_conv_w"][l],
                       W["ffn_conv_b"][l][None], W["ffn_down_bf"], l, conv_in[l], tm=tm_ffn, stride=stride,
                       out_gain=W["norm_final"][None] if l == DEPTH - 1 else None)
        conv_out.append(cbuf)
    return (x, jnp.stack(k_rows), jnp.stack(v_rows), jnp.stack(wkv_out), jnp.stack(shift_out),
            jnp.stack(pool_out), jnp.stack(conv_out))


def kernel(x_prompt, x_sample, cache_k, cache_v, state_wkv, state_shift, state_pool, state_ffn_conv,
           page_table, norm_mix, norm_ffn, norm_final, w_in, w_out, diff_lambda, diff_subln,
           rwkv_mu, rwkv_w0, rwkv_w2, rwkv_a0, rwkv_a2, rwkv_g2, rwkv_k_k, rwkv_k_a, rwkv_r_k,
           rwkv_gn_w, rwkv_gn_b, pool_w, pool_scale, ffn_up, ffn_conv_w, ffn_conv_b, ffn_down):
    W = dict(norm_mix=norm_mix, norm_ffn=norm_ffn, norm_final=norm_final,
             diff_lambda=diff_lambda, diff_subln=diff_subln, rwkv_mu=rwkv_mu, rwkv_w0=rwkv_w0,
             rwkv_w2=rwkv_w2, rwkv_a0=rwkv_a0, rwkv_a2=rwkv_a2, rwkv_g2=rwkv_g2, rwkv_k_k=rwkv_k_k,
             rwkv_k_a=rwkv_k_a, rwkv_r_k=rwkv_r_k, rwkv_gn_w=rwkv_gn_w, rwkv_gn_b=rwkv_gn_b,
             pool_scale=pool_scale, ffn_conv_w=ffn_conv_w, ffn_conv_b=ffn_conv_b,
             w_in_bf=w_in.astype(BF16), w_out_bf=w_out.astype(BF16), pool_w_bf=pool_w.astype(BF16),
             ffn_up_bf=ffn_up.astype(BF16), ffn_down_bf=ffn_down.astype(BF16))
    n_even, n_odd = state_wkv.shape[0], state_pool.shape[0]
    Bp, Sp, _ = x_prompt.shape
    Bs, Ts, _ = x_sample.shape
    n_pages = page_table.shape[1]
    past_len = n_pages * cache_k.shape[2]

    y_p, k_p, v_p, wkv_p, shift_p, pool_p, conv_p = _trunk(
        x_prompt, W, prompt=True, wkv_in=None,
        shift_in=jnp.zeros((n_even, Bp, 1, B_COLS), F32),
        pool_in=jnp.zeros((n_odd, Bp, POOL_BUF, D_MODEL), F32),
        conv_in=jnp.zeros((DEPTH, Bp, CONV_W - 1, D_FF), F32), attn_ctx=None)
    k_p = k_p.reshape(n_even, Bp, Sp, A_HEADS, 2 * A_HD)
    v_p = v_p.reshape(n_even, Bp, Sp, A_HEADS, 2 * A_HD)
    shift_p = shift_p.reshape(n_even, Bp, B_COLS)

    tmaj = lambda a: jnp.swapaxes(a, -3, -2)
    flat = lambda a: a.reshape(a.shape[:-3] + (1, a.shape[-3] * a.shape[-2], a.shape[-1]))
    ctx = dict(batch=Bs, past_len=past_len, page_table=page_table,
               cache_k=cache_k.reshape(cache_k.shape[:2] + (PAGE_SIZE * A_HEADS, 2 * A_HD)),
               cache_v=cache_v.reshape(cache_v.shape[:2] + (PAGE_SIZE * A_HEADS, 2 * A_HD)))
    y_s, k_s, v_s, wkv_s, shift_s, pool_s, conv_s = _trunk(
        flat(tmaj(x_sample)), W, prompt=False, wkv_in=state_wkv,
        shift_in=state_shift[:, None], pool_in=flat(tmaj(state_pool)),
        conv_in=flat(tmaj(state_ffn_conv)), attn_ctx=ctx)
    unflat = lambda a, j: tmaj(a.reshape(a.shape[:-3] + (j, Bs, a.shape[-1])))
    y_s = unflat(y_s, Ts)
    k_s = k_s.reshape(n_even, Ts, Bs, A_HEADS, 2 * A_HD).transpose(0, 2, 1, 3, 4)
    v_s = v_s.reshape(n_even, Ts, Bs, A_HEADS, 2 * A_HD).transpose(0, 2, 1, 3, 4)
    shift_s = shift_s.reshape(n_even, Bs, B_COLS)
    pool_s = unflat(pool_s, POOL_BUF)
    conv_s = unflat(conv_s, CONV_W - 1)
    return (y_p, y_s, k_p, v_p, wkv_p, shift_p, pool_p, conv_p,
            k_s, v_s, wkv_s, shift_s, pool_s, conv_s)
```
